```python
import jax, jax.numpy as jnp
from jax import lax
import numpy as np

D_MODEL = 2048
BATCH = 4
SEQ = 2048
DEPTH = 4

N_A = DEPTH // 2
N_B = DEPTH - N_A
D_FF = 4 * D_MODEL
CONV_WIDTH = 3
N_HEADS = 16
QK_NOPE = 128
QK_ROPE = 64
V_HEAD = 128
Q_LORA = 512
KV_LORA = 512
ROPE_THETA = 10000.0
Q_BLOCK = 128
ALPHA = (2 * DEPTH) ** 0.25
BETA = (8 * DEPTH) ** -0.25
LN_EPS = 1e-5
RMS_EPS = 1e-6

kernel_name = "yoco_shortconv_mla_deepnorm"


def layer_norm(x, g, b):
    xf = x.astype(jnp.float32)
    mu = jnp.mean(xf, axis=-1, keepdims=True)
    var = jnp.mean(jnp.square(xf - mu), axis=-1, keepdims=True)
    y = (xf - mu) * lax.rsqrt(var + LN_EPS) * g.astype(jnp.float32) + b.astype(jnp.float32)
    return y.astype(x.dtype)


def rms_norm(x, g):
    xf = x.astype(jnp.float32)
    y = xf * lax.rsqrt(jnp.mean(jnp.square(xf), axis=-1, keepdims=True) + RMS_EPS)
    return (y * g.astype(jnp.float32)).astype(x.dtype)


def rope_tables(seq, dim):
    inv = 1.0 / (ROPE_THETA ** (jnp.arange(0, dim, 2, dtype=jnp.float32) / dim))
    ang = jnp.arange(seq, dtype=jnp.float32)[:, None] * inv[None, :]
    return jnp.cos(ang), jnp.sin(ang)


def apply_rope(x, cos, sin):
    xf = x.astype(jnp.float32)
    half = xf.shape[-1] // 2
    x1, x2 = xf[..., :half], xf[..., half:]
    out = jnp.concatenate([x1 * cos - x2 * sin, x2 * cos + x1 * sin], axis=-1)
    return out.astype(x.dtype)


def short_conv_mixer(x, w_in, conv_w, w_out):
    bcu = x @ w_in
    gate_b, gate_c, u = jnp.split(bcu, 3, axis=-1)
    v = gate_c * u
    y = lax.conv_general_dilated(
        v, conv_w[:, None, :].astype(v.dtype),
        window_strides=(1,), padding=[(CONV_WIDTH - 1, 0)],
        dimension_numbers=("NWC", "WIO", "NWC"), feature_group_count=v.shape[-1])
    return (gate_b * y) @ w_out


def shared_latent_kv(h, w_dkv, kv_norm_g, w_ukv, cos, sin):
    bsz, seq, _ = h.shape
    ckv = h @ w_dkv
    c = rms_norm(ckv[..., :KV_LORA], kv_norm_g)
    k_pe = apply_rope(ckv[..., KV_LORA:], cos, sin)
    kv = (c @ w_ukv).reshape(bsz, seq, N_HEADS, QK_NOPE + V_HEAD)
    return kv[..., :QK_NOPE], k_pe, kv[..., QK_NOPE:]


def mla_mixer(x, w_dq, q_norm_g, w_uq, w_o, k_nope, k_pe, v, cos, sin):
    bsz, seq, _ = x.shape
    q = (rms_norm(x @ w_dq, q_norm_g) @ w_uq).reshape(bsz, seq, N_HEADS, QK_NOPE + QK_ROPE)
    q_nope = q[..., :QK_NOPE]
    q_pe = apply_rope(q[..., QK_NOPE:], cos[:, None, :], sin[:, None, :])
    nb = seq // Q_BLOCK
    scale = (QK_NOPE + QK_ROPE) ** -0.5
    k_pos = jnp.arange(seq)

    def to_blocks(t):
        return jnp.moveaxis(t.reshape(bsz, nb, Q_BLOCK, *t.shape[2:]), 1, 0)

    def attend_block(args):
        qn, qp, blk = args
        s = (jnp.einsum("bqhd,bkhd->bhqk", qn, k_nope)
             + jnp.einsum("bqhr,bkr->bhqk", qp, k_pe)).astype(jnp.float32) * scale
        q_pos = blk * Q_BLOCK + jnp.arange(Q_BLOCK)
        mask = k_pos[None, :] <= q_pos[:, None]
        s = jnp.where(mask[None, None], s, -jnp.inf)
        p = jax.nn.softmax(s, axis=-1).astype(v.dtype)
        return jnp.einsum("bhqk,bkhd->bqhd", p, v)

    o = lax.map(attend_block, (to_blocks(q_nope), to_blocks(q_pe), jnp.arange(nb)))
    o = jnp.moveaxis(o, 0, 1).reshape(bsz, seq, N_HEADS * V_HEAD)
    return o @ w_o


def sq_relu_mlp(x, w1, w2):
    return jnp.square(jax.nn.relu(x @ w1)) @ w2


def setup_inputs(seed: int = 0) -> dict:
    key = jax.random.key(seed)
    ks = jax.random.split(key, 16)
    f32 = jnp.float32

    def dense(k, shape, fan_in, scale=1.0):
        return jax.random.normal(k, shape, f32) * (scale * fan_in ** -0.5)

    return {
        "x": jax.random.normal(ks[0], (BATCH, SEQ, D_MODEL), f32),
        "ln_g": 1.0 + 0.02 * jax.random.normal(ks[1], (DEPTH, 2, D_MODEL), f32),
        "ln_b": 0.02 * jax.random.normal(ks[2], (DEPTH, 2, D_MODEL), f32),
        "conv_w_in": dense(ks[3], (N_A, D_MODEL, 3 * D_MODEL), D_MODEL),
        "conv_w": dense(ks[4], (N_A, CONV_WIDTH, D_MODEL), CONV_WIDTH),
        "conv_w_out": dense(ks[5], (N_A, D_MODEL, D_MODEL), D_MODEL, BETA),
        "kv_w_dkv": dense(ks[6], (D_MODEL, KV_LORA + QK_ROPE), D_MODEL),
        "kv_norm_g": 1.0 + 0.02 * jax.random.normal(ks[7], (KV_LORA,), f32),
        "kv_w_ukv": dense(ks[8], (KV_LORA, N_HEADS * (QK_NOPE + V_HEAD)), KV_LORA),
        "mla_w_dq": dense(ks[9], (N_B, D_MODEL, Q_LORA), D_MODEL),
        "mla_q_norm_g": 1.0 + 0.02 * jax.random.normal(ks[10], (N_B, Q_LORA), f32),
        "mla_w_uq": dense(ks[11], (N_B, Q_LORA, N_HEADS * (QK_NOPE + QK_ROPE)), Q_LORA),
        "mla_w_o": dense(ks[12], (N_B, N_HEADS * V_HEAD, D_MODEL), N_HEADS * V_HEAD, BETA),
        "mlp_w1": dense(ks[13], (DEPTH, D_MODEL, D_FF), D_MODEL),
        "mlp_w2": dense(ks[14], (DEPTH, D_FF, D_MODEL), D_FF, BETA),
    }


def reference(x, ln_g, ln_b, conv_w_in, conv_w, conv_w_out, kv_w_dkv, kv_norm_g,
              kv_w_ukv, mla_w_dq, mla_q_norm_g, mla_w_uq, mla_w_o, mlp_w1, mlp_w2):
    cos, sin = rope_tables(x.shape[1], QK_ROPE)
    h = x
    k_nope = k_pe = v = None
    for layer in range(DEPTH):
        if layer < N_A:
            mix = short_conv_mixer(h, conv_w_in[layer], conv_w[layer], conv_w_out[layer])
        else:
            if layer == N_A:
                k_nope, k_pe, v = shared_latent_kv(h, kv_w_dkv, kv_norm_g, kv_w_ukv, cos, sin)
            j = layer - N_A
            mix = mla_mixer(h, mla_w_dq[j], mla_q_norm_g[j], mla_w_uq[j], mla_w_o[j],
                            k_nope, k_pe, v, cos, sin)
        h = layer_norm(ALPHA * h + mix, ln_g[layer, 0], ln_b[layer, 0])
        h = layer_norm(ALPHA * h + sq_relu_mlp(h, mlp_w1[layer], mlp_w2[layer]),
                       ln_g[layer, 1], ln_b[layer, 1])
    return h
```

```python
import functools

import jax
import jax.numpy as jnp
from jax import lax
from jax.experimental import pallas as pl
from jax.experimental.pallas import tpu as pltpu

D_MODEL = 2048
DEPTH = 4
N_A = DEPTH // 2
D_FF = 4 * D_MODEL
CONV_WIDTH = 3
N_HEADS = 16
QK_NOPE = 128
QK_ROPE = 64
V_HEAD = 128
Q_LORA = 512
KV_LORA = 512
ROPE_THETA = 10000.0
ALPHA = (2 * DEPTH) ** 0.25
LN_EPS = 1e-5
RMS_EPS = 1e-6

LANES = 128
SUBLANES = 8
QK_PAD = 2 * LANES
VMEM_LIMIT = 56 * 1024 * 1024

F32 = jnp.float32
BF16 = jnp.bfloat16


def _params(n_axes):
    return pltpu.CompilerParams(
        dimension_semantics=("arbitrary",) * n_axes, vmem_limit_bytes=VMEM_LIMIT)


def _dot(a, b):
    return jnp.dot(a, b, preferred_element_type=F32)


def _layer_norm(y, g, b):
    mu = jnp.mean(y, axis=-1, keepdims=True)
    yc = y - mu
    var = jnp.mean(yc * yc, axis=-1, keepdims=True)
    return yc * lax.rsqrt(var + LN_EPS) * g + b


def _rms_norm(y, g):
    return y * lax.rsqrt(jnp.mean(y * y, axis=-1, keepdims=True) + RMS_EPS) * g


def _conv_in_kernel(h_ref, w_ref, cw_ref, z_ref, vbuf, *, tm, tn, tiles_per_seq):
    i = pl.program_id(1)
    x = h_ref[...].astype(BF16)
    bcu = _dot(x, w_ref[...])
    gate_b = bcu[:, :tn]
    v = bcu[:, tn:2 * tn] * bcu[:, 2 * tn:]

    @pl.when(i % tiles_per_seq == 0)
    def _():
        vbuf[0:SUBLANES, :] = jnp.zeros((SUBLANES, tn), F32)

    @pl.when(i % tiles_per_seq != 0)
    def _():
        vbuf[0:SUBLANES, :] = vbuf[tm:tm + SUBLANES, :]

    vbuf[SUBLANES:tm + SUBLANES, :] = v
    cw = cw_ref[...]
    y = (cw[2:3, :] * v
         + cw[1:2, :] * vbuf[SUBLANES - 1:tm + SUBLANES - 1, :]
         + cw[0:1, :] * vbuf[SUBLANES - 2:tm + SUBLANES - 2, :])
    z_ref[...] = (gate_b * y).astype(BF16)


def _conv_in(h, w_perm, conv_w, *, seq, tm=512, tn=512):
    m, d = h.shape
    kern = functools.partial(_conv_in_kernel, tm=tm, tn=tn, tiles_per_seq=seq // tm)
    return pl.pallas_call(
        kern,
        grid=(d // tn, m // tm),
        in_specs=[
            pl.BlockSpec((tm, d), lambda j, i: (i, 0)),
            pl.BlockSpec((d, 3 * tn), lambda j, i: (0, j)),
            pl.BlockSpec((CONV_WIDTH, tn), lambda j, i: (0, j)),
        ],
        out_specs=pl.BlockSpec((tm, tn), lambda j, i: (i, j)),
        out_shape=jax.ShapeDtypeStruct((m, d), BF16),
        scratch_shapes=[pltpu.VMEM((tm + SUBLANES, tn), F32)],
        compiler_params=_params(2),
        name="conv_in",
    )(h, w_perm, conv_w)


def _proj_ln_kernel(z_ref, w_ref, h_ref, g_ref, b_ref, o_ref):
    y = ALPHA * h_ref[...] + _dot(z_ref[...], w_ref[...])
    o_ref[...] = _layer_norm(y, g_ref[...], b_ref[...])


def _proj_ln(z, w, h, g, b, *, tm=512):
    m, k = z.shape
    d = w.shape[1]
    return pl.pallas_call(
        _proj_ln_kernel,
        grid=(m // tm,),
        in_specs=[
            pl.BlockSpec((tm, k), lambda i: (i, 0)),
            pl.BlockSpec((k, d), lambda i: (0, 0), pipeline_mode=pl.Buffered(1)),
            pl.BlockSpec((tm, d), lambda i: (i, 0)),
            pl.BlockSpec((1, d), lambda i: (0, 0)),
            pl.BlockSpec((1, d), lambda i: (0, 0)),
        ],
        out_specs=pl.BlockSpec((tm, d), lambda i: (i, 0)),
        out_shape=jax.ShapeDtypeStruct((m, d), F32),
        compiler_params=_params(1),
        name="proj_ln",
    )(z, w, h, g, b)


def _mlp_ln_kernel(h_ref, w1_ref, w2_ref, g_ref, b_ref, o_ref, xb_ref):
    k = pl.program_id(1)

    @pl.when(k == 0)
    def _():
        xb_ref[...] = h_ref[...].astype(BF16)

    a = jnp.maximum(_dot(xb_ref[...], w1_ref[...]), 0.0)
    part = _dot((a * a).astype(BF16), w2_ref[...])

    @pl.when(k == 0)
    def _():
        o_ref[...] = part

    @pl.when(k != 0)
    def _():
        o_ref[...] += part

    @pl.when(k == pl.num_programs(1) - 1)
    def _():
        y = ALPHA * h_ref[...] + o_ref[...]
        o_ref[...] = _layer_norm(y, g_ref[...], b_ref[...])


def _mlp_ln(h, w1, w2, g, b, *, tm=1024, tf=512):
    m, d = h.shape
    f = w1.shape[1]
    return pl.pallas_call(
        _mlp_ln_kernel,
        grid=(m // tm, f // tf),
        in_specs=[
            pl.BlockSpec((tm, d), lambda i, k: (i, 0), pipeline_mode=pl.Buffered(1)),
            pl.BlockSpec((d, tf), lambda i, k: (0, k)),
            pl.BlockSpec((tf, d), lambda i, k: (k, 0)),
            pl.BlockSpec((1, d), lambda i, k: (0, 0)),
            pl.BlockSpec((1, d), lambda i, k: (0, 0)),
        ],
        out_specs=pl.BlockSpec((tm, d), lambda i, k: (i, 0)),
        out_shape=jax.ShapeDtypeStruct((m, d), F32),
        scratch_shapes=[pltpu.VMEM((tm, d), BF16)],
        compiler_params=_params(2),
        name="mlp_ln",
    )(h, w1, w2, g, b)


def _kv_kernel(h_ref, wd_ref, g_ref, wu_ref, cos_ref, sin_ref, k_ref, v_ref):
    x = h_ref[...].astype(BF16)
    ckv = _dot(x, wd_ref[...])
    c = _rms_norm(ckv[:, :KV_LORA], g_ref[...])
    kv = _dot(c.astype(BF16), wu_ref[...])
    pe = (ckv[:, KV_LORA:KV_LORA + LANES] * cos_ref[...]
          + ckv[:, KV_LORA + LANES:] * sin_ref[...])
    lane = lax.broadcasted_iota(jnp.int32, pe.shape, 1)
    pe_lo = jnp.where(lane < QK_ROPE, pe, 0.0).astype(BF16)
    pe_hi = jnp.where(lane >= QK_ROPE, pe, 0.0).astype(BF16)
    for hd in range(N_HEADS):
        base = hd * (QK_NOPE + V_HEAD)
        k_ref[0, hd, :, :QK_NOPE] = kv[:, base:base + QK_NOPE].astype(BF16)
        k_ref[0, hd, :, QK_NOPE:] = pe_lo if hd % 2 == 0 else pe_hi
        v_ref[0, hd, :, :] = kv[:, base + QK_NOPE:base + QK_NOPE + V_HEAD].astype(BF16)


def _kv_proj(h, wd_ext, g, wu, cos_t, sin_t, *, bsz, seq, tm=256):
    m, d = h.shape
    tps = seq // tm
    return pl.pallas_call(
        _kv_kernel,
        grid=(m // tm,),
        in_specs=[
            pl.BlockSpec((tm, d), lambda i: (i, 0)),
            pl.BlockSpec(wd_ext.shape, lambda i: (0, 0), pipeline_mode=pl.Buffered(1)),
            pl.BlockSpec((1, KV_LORA), lambda i: (0, 0)),
            pl.BlockSpec(wu.shape, lambda i: (0, 0), pipeline_mode=pl.Buffered(1)),
            pl.BlockSpec((tm, LANES), lambda i: (i % tps, 0)),
            pl.BlockSpec((tm, LANES), lambda i: (i % tps, 0)),
        ],
        out_specs=[
            pl.BlockSpec((1, N_HEADS, tm, QK_PAD), lambda i: (i // tps, 0, i % tps, 0)),
            pl.BlockSpec((1, N_HEADS, tm, V_HEAD), lambda i: (i // tps, 0, i % tps, 0)),
        ],
        out_shape=[
            jax.ShapeDtypeStruct((bsz, N_HEADS, seq, QK_PAD), BF16),
            jax.ShapeDtypeStruct((bsz, N_HEADS, seq, V_HEAD), BF16),
        ],
        compiler_params=_params(1),
        name="kv_proj",
    )(h, wd_ext, g, wu, cos_t, sin_t)


def _q_kernel(h_ref, wd_ref, g_ref, wu_ref, cos_ref, sin_ref, q_ref):
    x = h_ref[...].astype(BF16)
    qc = _rms_norm(_dot(x, wd_ref[...]), g_ref[...])
    q = _dot(qc.astype(BF16), wu_ref[...])
    nope_w = N_HEADS * QK_NOPE
    pe_w = N_HEADS * QK_ROPE
    cos = cos_ref[...]
    sin = sin_ref[...]
    for pair in range(N_HEADS // 2):
        lo = nope_w + pair * LANES
        pe = (q[:, lo:lo + LANES] * cos + q[:, lo + pe_w:lo + pe_w + LANES] * sin).astype(BF16)
        for hd in (2 * pair, 2 * pair + 1):
            q_ref[0, hd, :, :QK_NOPE] = q[:, hd * QK_NOPE:(hd + 1) * QK_NOPE].astype(BF16)
            q_ref[0, hd, :, QK_NOPE:] = pe


def _q_proj(h, wd, g, wu_ext, cos_t, sin_t, *, bsz, seq, tm=256):
    m, d = h.shape
    tps = seq // tm
    return pl.pallas_call(
        _q_kernel,
        grid=(m // tm,),
        in_specs=[
            pl.BlockSpec((tm, d), lambda i: (i, 0)),
            pl.BlockSpec(wd.shape, lambda i: (0, 0), pipeline_mode=pl.Buffered(1)),
            pl.BlockSpec((1, Q_LORA), lambda i: (0, 0)),
            pl.BlockSpec(wu_ext.shape, lambda i: (0, 0), pipeline_mode=pl.Buffered(1)),
            pl.BlockSpec((tm, LANES), lambda i: (i % tps, 0)),
            pl.BlockSpec((tm, LANES), lambda i: (i % tps, 0)),
        ],
        out_specs=pl.BlockSpec((1, N_HEADS, tm, QK_PAD), lambda i: (i // tps, 0, i % tps, 0)),
        out_shape=jax.ShapeDtypeStruct((bsz, N_HEADS, seq, QK_PAD), BF16),
        compiler_params=_params(1),
        name="q_proj",
    )(h, wd, g, wu_ext, cos_t, sin_t)


def _attn_kernel(q_ref, k_ref, v_ref, o_ref, m_ref, l_ref, acc_ref, *, tq, scale):
    qi = pl.program_id(2)
    q = q_ref[0, 0]
    m_ref[...] = jnp.full(m_ref.shape, -jnp.inf, F32)
    l_ref[...] = jnp.zeros(l_ref.shape, F32)
    acc_ref[...] = jnp.zeros(acc_ref.shape, F32)

    def block(j, masked):
        start = pl.multiple_of(j * tq, tq)
        k = k_ref[0, 0, pl.ds(start, tq), :]
        v = v_ref[0, 0, pl.ds(start, tq), :]
        s = lax.dot_general(q, k, (((1,), (1,)), ((), ())), preferred_element_type=F32) * scale
        if masked:
            row = lax.broadcasted_iota(jnp.int32, s.shape, 0)
            col = lax.broadcasted_iota(jnp.int32, s.shape, 1)
            s = jnp.where(col <= row, s, -jnp.inf)
        m_prev = m_ref[...]
        m_new = jnp.maximum(m_prev, jnp.max(s, axis=-1, keepdims=True))
        corr = jnp.exp(m_prev - m_new)
        p = jnp.exp(s - m_new)
        l_ref[...] = corr * l_ref[...] + jnp.sum(p, axis=-1, keepdims=True)
        acc_ref[...] = corr * acc_ref[...] + _dot(p.astype(BF16), v)
        m_ref[...] = m_new

    def body(j, carry):
        block(j, masked=False)
        return carry

    lax.fori_loop(0, qi, body, 0)
    block(qi, masked=True)
    o_ref[0] = (acc_ref[...] / l_ref[...]).astype(BF16)


def _attention(q, k, v, *, tq=512):
    bsz, nh, seq, _ = q.shape
    scale = (QK_NOPE + QK_ROPE) ** -0.5
    kern = functools.partial(_attn_kernel, tq=tq, scale=scale)
    return pl.pallas_call(
        kern,
        grid=(bsz, nh, seq // tq),
        in_specs=[
            pl.BlockSpec((1, 1, tq, QK_PAD), lambda b, h, i: (b, h, i, 0)),
            pl.BlockSpec((1, 1, seq, QK_PAD), lambda b, h, i: (b, h, 0, 0)),
            pl.BlockSpec((1, 1, seq, V_HEAD), lambda b, h, i: (b, h, 0, 0)),
        ],
        out_specs=pl.BlockSpec((1, tq, V_HEAD), lambda b, h, i: (b, i, h)),
        out_shape=jax.ShapeDtypeStruct((bsz, seq, nh * V_HEAD), BF16),
        scratch_shapes=[
            pltpu.VMEM((tq, 1), F32),
            pltpu.VMEM((tq, 1), F32),
            pltpu.VMEM((tq, V_HEAD), F32),
        ],
        compiler_params=_params(3),
        name="attention",
    )(q, k, v)


def _rotate_half_cols(w):
    half = w.shape[-1] // 2
    return jnp.concatenate([-w[..., half:], w[..., :half]], axis=-1)


def _rope_tables(seq):
    inv = 1.0 / (ROPE_THETA ** (jnp.arange(0, QK_ROPE, 2, dtype=F32) / QK_ROPE))
    ang = jnp.arange(seq, dtype=F32)[:, None] * inv[None, :]
    cos, sin = jnp.cos(ang), jnp.sin(ang)
    return jnp.tile(cos, (1, 4)), jnp.tile(sin, (1, 4))


def kernel(x, ln_g, ln_b, conv_w_in, conv_w, conv_w_out, kv_w_dkv, kv_norm_g, kv_w_ukv,
           mla_w_dq, mla_q_norm_g, mla_w_uq, mla_w_o, mlp_w1, mlp_w2):
    bsz, seq, d = x.shape
    m = bsz * seq
    h = x.reshape(m, d)
    cos_t, sin_t = _rope_tables(seq)
    conv_tn = 512

    k_all = v_all = None
    for layer in range(DEPTH):
        g0, b0 = ln_g[layer, 0][None, :], ln_b[layer, 0][None, :]
        g1, b1 = ln_g[layer, 1][None, :], ln_b[layer, 1][None, :]
        if layer < N_A:
            w_perm = (conv_w_in[layer].reshape(d, 3, d // conv_tn, conv_tn)
                      .transpose(0, 2, 1, 3).reshape(d, 3 * d).astype(BF16))
            z = _conv_in(h, w_perm, conv_w[layer], seq=seq, tn=conv_tn)
            h = _proj_ln(z, conv_w_out[layer].astype(BF16), h, g0, b0)
        else:
            if layer == N_A:
                pe_w = kv_w_dkv[:, KV_LORA:]
                rot_w = _rotate_half_cols(pe_w)
                wd_ext = jnp.concatenate(
                    [kv_w_dkv[:, :KV_LORA], pe_w, pe_w, rot_w, rot_w], axis=1).astype(BF16)
                k_all, v_all = _kv_proj(h, wd_ext, kv_norm_g[None, :], kv_w_ukv.astype(BF16),
                                        cos_t, sin_t, bsz=bsz, seq=seq)
            j = layer - N_A
            wu = mla_w_uq[j].reshape(Q_LORA, N_HEADS, QK_NOPE + QK_ROPE)
            wu_pe = wu[:, :, QK_NOPE:]
            wu_ext = jnp.concatenate(
                [wu[:, :, :QK_NOPE].reshape(Q_LORA, -1), wu_pe.reshape(Q_LORA, -1),
                 _rotate_half_cols(wu_pe).reshape(Q_LORA, -1)], axis=1).astype(BF16)
            q_all = _q_proj(h, mla_w_dq[j].astype(BF16), mla_q_norm_g[j][None, :], wu_ext,
                            cos_t, sin_t, bsz=bsz, seq=seq)
            o = _attention(q_all, k_all, v_all)
            h = _proj_ln(o.reshape(m, N_HEADS * V_HEAD), mla_w_o[j].astype(BF16), h, g0, b0)
        h = _mlp_ln(h, mlp_w1[layer].astype(BF16), mlp_w2[layer].astype(BF16), g1, b1)
    return h.reshape(bsz, seq, d)
```

```python
import functools
import math

import jax
import jax.numpy as jnp
from jax import lax
from jax.experimental import pallas as pl
from jax.experimental.pallas import tpu as pltpu

D_MODEL = 2048
DEPTH = 4
N_A = DEPTH // 2
D_FF = 4 * D_MODEL
CONV_WIDTH = 3
N_HEADS = 16
QK_NOPE = 128
QK_ROPE = 64
V_HEAD = 128
Q_LORA = 512
KV_LORA = 512
ROPE_THETA = 10000.0
ALPHA = (2 * DEPTH) ** 0.25
LN_EPS = 1e-5
RMS_EPS = 1e-6

LANES = 128
SUBLANES = 8
MXU_DIM = 256
QK_PAD = 2 * LANES
VMEM_LIMIT = 56 * 1024 * 1024

ATTN_TQ = 1024
ATTN_TK = 512

F32 = jnp.float32
BF16 = jnp.bfloat16
NT_DIMS = (((1,), (1,)), ((), ()))


def _params(n_axes):
    return pltpu.CompilerParams(
        dimension_semantics=("arbitrary",) * n_axes, vmem_limit_bytes=VMEM_LIMIT)


def _dot(a, b):
    return jnp.dot(a, b, preferred_element_type=F32)


def _dot_nt(a, b):
    return lax.dot_general(a, b, NT_DIMS, preferred_element_type=F32)


def _layer_norm(y, g, b):
    mu = jnp.mean(y, axis=-1, keepdims=True)
    yc = y - mu
    var = jnp.mean(yc * yc, axis=-1, keepdims=True)
    return yc * lax.rsqrt(var + LN_EPS) * g + b


def _rms_norm(y, g):
    return y * lax.rsqrt(jnp.mean(y * y, axis=-1, keepdims=True) + RMS_EPS) * g


def _conv_in_kernel(h_ref, wb_ref, wc_ref, wu_ref, cw_ref, z_ref, w_bf, vbuf,
                    *, tm, tn, tiles_per_seq):
    i = pl.program_id(1)

    @pl.when(i == 0)
    def _():
        w_bf[:, :tn] = wb_ref[...].astype(BF16)
        w_bf[:, tn:2 * tn] = wc_ref[...].astype(BF16)
        w_bf[:, 2 * tn:] = wu_ref[...].astype(BF16)

    x = h_ref[...].astype(BF16)
    bcu = _dot(x, w_bf[...])
    gate_b = bcu[:, :tn]
    v = bcu[:, tn:2 * tn] * bcu[:, 2 * tn:]

    @pl.when(i % tiles_per_seq == 0)
    def _():
        vbuf[0:SUBLANES, :] = jnp.zeros((SUBLANES, tn), F32)

    @pl.when(i % tiles_per_seq != 0)
    def _():
        vbuf[0:SUBLANES, :] = vbuf[tm:tm + SUBLANES, :]

    vbuf[SUBLANES:tm + SUBLANES, :] = v
    cw = cw_ref[...]
    y = (cw[2:3, :] * v
         + cw[1:2, :] * vbuf[SUBLANES - 1:tm + SUBLANES - 1, :]
         + cw[0:1, :] * vbuf[SUBLANES - 2:tm + SUBLANES - 2, :])
    z_ref[...] = (gate_b * y).astype(BF16)


def _conv_in(h, w_in, conv_w, layer, *, seq, tm=512, tn=512):
    m, d = h.shape
    nj = d // tn
    kern = functools.partial(_conv_in_kernel, tm=tm, tn=tn, tiles_per_seq=seq // tm)
    w_spec = [pl.BlockSpec((None, d, tn), lambda j, i, g=g: (layer, 0, g * nj + j))
              for g in range(3)]
    return pl.pallas_call(
        kern,
        grid=(nj, m // tm),
        in_specs=[pl.BlockSpec((tm, d), lambda j, i: (i, 0))] + w_spec + [
            pl.BlockSpec((None, CONV_WIDTH, tn), lambda j, i: (layer, 0, j))],
        out_specs=pl.BlockSpec((tm, tn), lambda j, i: (i, j)),
        out_shape=jax.ShapeDtypeStruct((m, d), BF16),
        scratch_shapes=[pltpu.VMEM((d, 3 * tn), BF16),
                        pltpu.VMEM((tm + SUBLANES, tn), F32)],
        compiler_params=_params(2),
        name="conv_in",
    )(h, w_in, w_in, w_in, conv_w)


def _proj_ln_kernel(z_ref, w_ref, h_ref, g_ref, b_ref, o_ref, w_bf):
    @pl.when(pl.program_id(0) == 0)
    def _():
        w_bf[...] = w_ref[...].astype(BF16)

    y = ALPHA * h_ref[...] + _dot(z_ref[...], w_bf[...])
    o_ref[...] = _layer_norm(y, g_ref[...], b_ref[...])


def _proj_ln(z, w, w_layer, h, ln_g, ln_b, ln_row, *, tm=512):
    m, k = z.shape
    d = w.shape[-1]
    return pl.pallas_call(
        _proj_ln_kernel,
        grid=(m // tm,),
        in_specs=[
            pl.BlockSpec((tm, k), lambda i: (i, 0)),
            pl.BlockSpec((None, k, d), lambda i: (w_layer, 0, 0), pipeline_mode=pl.Buffered(1)),
            pl.BlockSpec((tm, d), lambda i: (i, 0)),
            pl.BlockSpec((None, 1, d), lambda i: (ln_row, 0, 0)),
            pl.BlockSpec((None, 1, d), lambda i: (ln_row, 0, 0)),
        ],
        out_specs=pl.BlockSpec((tm, d), lambda i: (i, 0)),
        out_shape=jax.ShapeDtypeStruct((m, d), F32),
        scratch_shapes=[pltpu.VMEM((k, d), BF16)],
        compiler_params=_params(1),
        name="proj_ln",
    )(z, w, h, ln_g, ln_b)


def _mlp_ln_kernel(h_ref, w1_ref, w2_ref, g_ref, b_ref, o_ref, xb_ref):
    k = pl.program_id(1)

    @pl.when(k == 0)
    def _():
        xb_ref[...] = h_ref[...].astype(BF16)
        o_ref[...] = jnp.zeros(o_ref.shape, F32)

    a = jnp.maximum(_dot(xb_ref[...], w1_ref[...]), 0.0)
    o_ref[...] += _dot((a * a).astype(BF16), w2_ref[...])

    @pl.when(k == pl.num_programs(1) - 1)
    def _():
        y = ALPHA * h_ref[...] + o_ref[...]
        o_ref[...] = _layer_norm(y, g_ref[...], b_ref[...])


def _mlp_ln(h, w1, w2, layer, ln_g, ln_b, *, tm=1024, tf=512):
    m, d = h.shape
    f = w1.shape[-1]
    ln_row = 2 * layer + 1
    return pl.pallas_call(
        _mlp_ln_kernel,
        grid=(m // tm, f // tf),
        in_specs=[
            pl.BlockSpec((tm, d), lambda i, k: (i, 0), pipeline_mode=pl.Buffered(1)),
            pl.BlockSpec((None, d, tf), lambda i, k: (layer, 0, k)),
            pl.BlockSpec((None, tf, d), lambda i, k: (layer, k, 0)),
            pl.BlockSpec((None, 1, d), lambda i, k: (ln_row, 0, 0)),
            pl.BlockSpec((None, 1, d), lambda i, k: (ln_row, 0, 0)),
        ],
        out_specs=pl.BlockSpec((tm, d), lambda i, k: (i, 0)),
        out_shape=jax.ShapeDtypeStruct((m, d), F32),
        scratch_shapes=[pltpu.VMEM((tm, d), BF16)],
        compiler_params=_params(2),
        name="mlp_ln",
    )(h, w1, w2, ln_g, ln_b)


def _kv_kernel(h_ref, wd_ref, g_ref, wuk_ref, wuvt_ref, cos_ref, sin_ref, k_ref, v_ref):
    x = h_ref[...].astype(BF16)
    ckv = _dot(x, wd_ref[...])
    c = _rms_norm(ckv[:, :KV_LORA], g_ref[...]).astype(BF16)
    kn = _dot(c, wuk_ref[...])
    vt = _dot_nt(wuvt_ref[...], c)
    pe = (ckv[:, KV_LORA:KV_LORA + LANES] * cos_ref[...]
          + ckv[:, KV_LORA + LANES:] * sin_ref[...])
    lane = lax.broadcasted_iota(jnp.int32, pe.shape, 1)
    pe_lo = jnp.where(lane < QK_ROPE, pe, 0.0).astype(BF16)
    pe_hi = jnp.where(lane >= QK_ROPE, pe, 0.0).astype(BF16)
    for hd in range(N_HEADS):
        k_ref[0, hd, :, :QK_NOPE] = kn[:, hd * QK_NOPE:(hd + 1) * QK_NOPE].astype(BF16)
        k_ref[0, hd, :, QK_NOPE:] = pe_lo if hd % 2 == 0 else pe_hi
        v_ref[0, hd, 0] = vt[hd * V_HEAD:(hd + 1) * V_HEAD, :].astype(BF16)


def _kv_proj(h, wd_ext, g, wuk, wuvt, cos_t, sin_t, *, bsz, seq):
    m, d = h.shape
    tm = ATTN_TK
    tps = seq // tm
    return pl.pallas_call(
        _kv_kernel,
        grid=(m // tm,),
        in_specs=[
            pl.BlockSpec((tm, d), lambda i: (i, 0)),
            pl.BlockSpec(wd_ext.shape, lambda i: (0, 0), pipeline_mode=pl.Buffered(1)),
            pl.BlockSpec((1, KV_LORA), lambda i: (0, 0)),
            pl.BlockSpec(wuk.shape, lambda i: (0, 0), pipeline_mode=pl.Buffered(1)),
            pl.BlockSpec(wuvt.shape, lambda i: (0, 0), pipeline_mode=pl.Buffered(1)),
            pl.BlockSpec((tm, LANES), lambda i: (i % tps, 0)),
            pl.BlockSpec((tm, LANES), lambda i: (i % tps, 0)),
        ],
        out_specs=[
            pl.BlockSpec((1, N_HEADS, tm, QK_PAD), lambda i: (i // tps, 0, i % tps, 0)),
            pl.BlockSpec((1, N_HEADS, 1, V_HEAD, tm), lambda i: (i // tps, 0, i % tps, 0, 0)),
        ],
        out_shape=[
            jax.ShapeDtypeStruct((bsz, N_HEADS, seq, QK_PAD), BF16),
            jax.ShapeDtypeStruct((bsz, N_HEADS, tps, V_HEAD, tm), BF16),
        ],
        compiler_params=_params(1),
        name="kv_proj",
    )(h, wd_ext, g, wuk, wuvt, cos_t, sin_t)


def _q_kernel(h_ref, wd_ref, g_ref, wut_ref, cos_ref, sin_ref, q_ref):
    x = h_ref[...].astype(BF16)
    qc = _rms_norm(_dot(x, wd_ref[...]), g_ref[...]).astype(BF16)
    qt = _dot_nt(wut_ref[...], qc)
    nope_w = N_HEADS * QK_NOPE
    pe_w = N_HEADS * QK_ROPE
    cos = cos_ref[...]
    sin = sin_ref[...]
    for pair in range(N_HEADS // 2):
        lo = nope_w + pair * LANES
        pe = (qt[lo:lo + LANES, :] * cos + qt[lo + pe_w:lo + pe_w + LANES, :] * sin).astype(BF16)
        for hd in (2 * pair, 2 * pair + 1):
            q_ref[0, hd, :QK_NOPE, :] = qt[hd * QK_NOPE:(hd + 1) * QK_NOPE, :].astype(BF16)
            q_ref[0, hd, QK_NOPE:, :] = pe


def _q_proj(h, wd, g, wut_ext, cos_tt, sin_tt, *, bsz, seq, tm=256):
    m, d = h.shape
    tps = seq // tm
    return pl.pallas_call(
        _q_kernel,
        grid=(m // tm,),
        in_specs=[
            pl.BlockSpec((tm, d), lambda i: (i, 0)),
            pl.BlockSpec(wd.shape, lambda i: (0, 0), pipeline_mode=pl.Buffered(1)),
            pl.BlockSpec((1, Q_LORA), lambda i: (0, 0)),
            pl.BlockSpec(wut_ext.shape, lambda i: (0, 0), pipeline_mode=pl.Buffered(1)),
            pl.BlockSpec((LANES, tm), lambda i: (0, i % tps)),
            pl.BlockSpec((LANES, tm), lambda i: (0, i % tps)),
        ],
        out_specs=pl.BlockSpec((1, N_HEADS, QK_PAD, tm), lambda i: (i // tps, 0, 0, i % tps)),
        out_shape=jax.ShapeDtypeStruct((bsz, N_HEADS, QK_PAD, seq), BF16),
        compiler_params=_params(1),
        name="q_proj",
    )(h, wd, g, wut_ext, cos_tt, sin_tt)


def _attn_kernel(q_ref, k_ref, v_ref, o_ref, m_ref, l_ref, acc_ref, *, tq, tk, c):
    qi = pl.program_id(2)
    cw = MXU_DIM
    n_chain = tq // cw
    kv_per_q = tq // tk

    m_ref[...] = jnp.full(m_ref.shape, -jnp.inf, F32)
    l_ref[...] = jnp.zeros(l_ref.shape, F32)
    acc_ref[...] = jnp.zeros(acc_ref.shape, F32)

    def scores(k, ch):
        return _dot(k, q_ref[0, 0, :, ch * cw:(ch + 1) * cw])

    def chain_step(state, s, vt, mask_shift):
        m_prev, l_prev, acc_prev = state
        if mask_shift is not None:
            row = lax.broadcasted_iota(jnp.int32, s.shape, 0)
            col = lax.broadcasted_iota(jnp.int32, s.shape, 1)
            s = jnp.where(row <= col + mask_shift, s, -jnp.inf)
        m_new = jnp.maximum(m_prev, jnp.max(s, axis=0, keepdims=True))
        p = jnp.exp2(s * c - m_new * c)
        corr = jnp.exp2((m_prev - m_new) * c)
        l_new = corr * l_prev + jnp.sum(p, axis=0, keepdims=True)
        acc_new = corr * acc_prev + _dot(vt, p.astype(BF16))
        return m_new, l_new, acc_new

    def load_state(ch):
        sl = slice(ch * cw, (ch + 1) * cw)
        return m_ref[:, sl], l_ref[:, sl], acc_ref[:, sl]

    def store_state(ch, state):
        sl = slice(ch * cw, (ch + 1) * cw)
        m_ref[:, sl], l_ref[:, sl], acc_ref[:, sl] = state

    def kv_block(j):
        k = k_ref[0, 0, pl.ds(pl.multiple_of(j * tk, tk), tk), :]
        return k, v_ref[0, 0, j]

    def run(first_kv, blocks):
        kv = [kv_block(first_kv + u) for u in range(kv_per_q)]
        s_all = [scores(kv[u][0], ch) for u, ch, _ in blocks]
        states = [load_state(ch) for ch in range(n_chain)]
        for (u, ch, shift), s in zip(blocks, s_all):
            states[ch] = chain_step(states[ch], s, kv[u][1], shift)
        return states

    def body(t, carry):
        states = run(t * kv_per_q,
                     [(u, ch, None) for u in range(kv_per_q) for ch in range(n_chain)])
        for ch in range(n_chain):
            store_state(ch, states[ch])
        return carry

    lax.fori_loop(0, qi, body, 0)

    diag = []
    for u in range(kv_per_q):
        for ch in range(n_chain):
            shift = ch * cw - u * tk
            if shift >= 0:
                diag.append((u, ch, None if shift >= tk - 1 else shift))
    states = run(qi * kv_per_q, diag)

    for ch in range(n_chain):
        _, l_fin, acc_fin = states[ch]
        out = (acc_fin * (1.0 / l_fin)).T
        o_ref[0, ch * cw:(ch + 1) * cw, :] = out.astype(BF16)


def _attention(qt, k, vt):
    bsz, nh, _, seq = qt.shape
    tq, tk = ATTN_TQ, ATTN_TK
    c = (QK_NOPE + QK_ROPE) ** -0.5 * math.log2(math.e)
    kern = functools.partial(_attn_kernel, tq=tq, tk=tk, c=c)
    return pl.pallas_call(
        kern,
        grid=(bsz, nh, seq // tq),
        in_specs=[
            pl.BlockSpec((1, 1, QK_PAD, tq), lambda b, h, i: (b, h, 0, i)),
            pl.BlockSpec((1, 1, seq, QK_PAD), lambda b, h, i: (b, h, 0, 0)),
            pl.BlockSpec((1, 1, seq // tk, V_HEAD, tk), lambda b, h, i: (b, h, 0, 0, 0)),
        ],
        out_specs=pl.BlockSpec((1, tq, V_HEAD), lambda b, h, i: (b, i, h)),
        out_shape=jax.ShapeDtypeStruct((bsz, seq, nh * V_HEAD), BF16),
        scratch_shapes=[
            pltpu.VMEM((1, tq), F32),
            pltpu.VMEM((1, tq), F32),
            pltpu.VMEM((V_HEAD, tq), F32),
        ],
        compiler_params=_params(3),
        name="attention",
    )(qt, k, vt)


def _rotate_half_cols(w):
    half = w.shape[-1] // 2
    return jnp.concatenate([-w[..., half:], w[..., :half]], axis=-1)


def _rope_tables(seq):
    inv = 1.0 / (ROPE_THETA ** (jnp.arange(0, QK_ROPE, 2, dtype=F32) / QK_ROPE))
    ang = jnp.arange(seq, dtype=F32)[:, None] * inv[None, :]
    cos, sin = jnp.cos(ang), jnp.sin(ang)
    return jnp.tile(cos, (1, 4)), jnp.tile(sin, (1, 4))


def kernel(x, ln_g, ln_b, conv_w_in, conv_w, conv_w_out, kv_w_dkv, kv_norm_g, kv_w_ukv,
           mla_w_dq, mla_q_norm_g, mla_w_uq, mla_w_o, mlp_w1, mlp_w2):
    bsz, seq, d = x.shape
    m = bsz * seq
    h = x.reshape(m, d)
    cos_t, sin_t = _rope_tables(seq)
    cos_tt, sin_tt = cos_t.T, sin_t.T
    ln_g = ln_g.reshape(2 * DEPTH, 1, d)
    ln_b = ln_b.reshape(2 * DEPTH, 1, d)
    w1_bf = mlp_w1.astype(BF16)
    w2_bf = mlp_w2.astype(BF16)

    k_all = vt_all = None
    for layer in range(DEPTH):
        if layer < N_A:
            z = _conv_in(h, conv_w_in, conv_w, layer, seq=seq)
            h = _proj_ln(z, conv_w_out, layer, h, ln_g, ln_b, 2 * layer)
        else:
            if layer == N_A:
                pe_w = kv_w_dkv[:, KV_LORA:]
                rot_w = _rotate_half_cols(pe_w)
                wd_ext = jnp.concatenate(
                    [kv_w_dkv[:, :KV_LORA], pe_w, pe_w, rot_w, rot_w], axis=1).astype(BF16)
                wu = kv_w_ukv.reshape(KV_LORA, N_HEADS, QK_NOPE + V_HEAD)
                wuk = wu[:, :, :QK_NOPE].reshape(KV_LORA, -1).astype(BF16)
                wuvt = wu[:, :, QK_NOPE:].reshape(KV_LORA, -1).T.astype(BF16)
                k_all, vt_all = _kv_proj(h, wd_ext, kv_norm_g[None, :], wuk, wuvt,
                                         cos_t, sin_t, bsz=bsz, seq=seq)
            j = layer - N_A
            wu = mla_w_uq[j].reshape(Q_LORA, N_HEADS, QK_NOPE + QK_ROPE)
            wu_pe = wu[:, :, QK_NOPE:]
            wut_ext = jnp.concatenate(
                [wu[:, :, :QK_NOPE].reshape(Q_LORA, -1), wu_pe.reshape(Q_LORA, -1),
                 _rotate_half_cols(wu_pe).reshape(Q_LORA, -1)], axis=1).T.astype(BF16)
            qt_all = _q_proj(h, mla_w_dq[j].astype(BF16), mla_q_norm_g[j][None, :], wut_ext,
                             cos_tt, sin_tt, bsz=bsz, seq=seq)
            o = _attention(qt_all, k_all, vt_all)
            h = _proj_ln(o.reshape(m, N_HEADS * V_HEAD), mla_w_o, j, h, ln_g, ln_b, 2 * layer)
        h = _mlp_ln(h, w1_bf, w2_bf, layer, ln_g, ln_b)
    return h.reshape(bsz, seq, d)
```

```python
import functools
import math

import jax
import jax.numpy as jnp
from jax import lax
from jax.experimental import pallas as pl
from jax.experimental.pallas import tpu as pltpu

D_MODEL = 2048
DEPTH = 4
N_A = DEPTH // 2
D_FF = 4 * D_MODEL
CONV_WIDTH = 3
N_HEADS = 16
QK_NOPE = 128
QK_ROPE = 64
V_HEAD = 128
Q_LORA = 512
KV_LORA = 512
ROPE_THETA = 10000.0
ALPHA = (2 * DEPTH) ** 0.25
LN_EPS = 1e-5
RMS_EPS = 1e-6

LANES = 128
SUBLANES = 8
MXU_DIM = 256
QK_PAD = 2 * LANES
VMEM_LIMIT = 56 * 1024 * 1024

ATTN_TQ = 1024
ATTN_TK = 512

F32 = jnp.float32
BF16 = jnp.bfloat16
NT_DIMS = (((1,), (1,)), ((), ()))


def _params(n_axes):
    return pltpu.CompilerParams(
        dimension_semantics=("arbitrary",) * n_axes, vmem_limit_bytes=VMEM_LIMIT)


def _dot(a, b):
    return jnp.dot(a, b, preferred_element_type=F32)


def _dot_nt(a, b):
    return lax.dot_general(a, b, NT_DIMS, preferred_element_type=F32)


def _layer_norm(y, g, b):
    mu = jnp.mean(y, axis=-1, keepdims=True)
    yc = y - mu
    var = jnp.mean(yc * yc, axis=-1, keepdims=True)
    return yc * lax.rsqrt(var + LN_EPS) * g + b


def _rms_norm(y, g):
    return y * lax.rsqrt(jnp.mean(y * y, axis=-1, keepdims=True) + RMS_EPS) * g


def _conv_in_kernel(h_ref, wb_ref, wc_ref, wu_ref, cw_ref, z_ref, w_bf, vbuf,
                    *, tm, tn, tiles_per_seq, n_sub):
    i = pl.program_id(1)

    @pl.when(i == 0)
    def _():
        w_bf[:, :tn] = wb_ref[...].astype(BF16)
        w_bf[:, tn:2 * tn] = wc_ref[...].astype(BF16)
        w_bf[:, 2 * tn:] = wu_ref[...].astype(BF16)

    @pl.when(i % tiles_per_seq == 0)
    def _():
        vbuf[0:SUBLANES, :] = jnp.zeros((SUBLANES, tn), F32)

    @pl.when(i % tiles_per_seq != 0)
    def _():
        vbuf[0:SUBLANES, :] = vbuf[tm:tm + SUBLANES, :]

    ts = tm // n_sub
    w = w_bf[...]
    bcu = [_dot(h_ref[s * ts:(s + 1) * ts, :].astype(BF16), w) for s in range(n_sub)]
    cw = cw_ref[...]
    for s in range(n_sub):
        r0 = s * ts + SUBLANES
        gate_b = bcu[s][:, :tn]
        v = bcu[s][:, tn:2 * tn] * bcu[s][:, 2 * tn:]
        vbuf[r0:r0 + ts, :] = v
        y = (cw[2:3, :] * v
             + cw[1:2, :] * vbuf[r0 - 1:r0 + ts - 1, :]
             + cw[0:1, :] * vbuf[r0 - 2:r0 + ts - 2, :])
        z_ref[s * ts:(s + 1) * ts, :] = (gate_b * y).astype(BF16)


def _conv_in(h, w_in, conv_w, layer, *, seq, tm=512, tn=512, n_sub=2):
    m, d = h.shape
    nj = d // tn
    kern = functools.partial(_conv_in_kernel, tm=tm, tn=tn, tiles_per_seq=seq // tm, n_sub=n_sub)
    w_spec = [pl.BlockSpec((None, d, tn), lambda j, i, g=g: (layer, 0, g * nj + j))
              for g in range(3)]
    return pl.pallas_call(
        kern,
        grid=(nj, m // tm),
        in_specs=[pl.BlockSpec((tm, d), lambda j, i: (i, 0))] + w_spec + [
            pl.BlockSpec((None, CONV_WIDTH, tn), lambda j, i: (layer, 0, j))],
        out_specs=pl.BlockSpec((tm, tn), lambda j, i: (i, j)),
        out_shape=jax.ShapeDtypeStruct((m, d), BF16),
        scratch_shapes=[pltpu.VMEM((d, 3 * tn), BF16),
                        pltpu.VMEM((tm + SUBLANES, tn), F32)],
        compiler_params=_params(2),
        name="conv_in",
    )(h, w_in, w_in, w_in, conv_w)


def _proj_ln_kernel(z_ref, w_ref, h_ref, g_ref, b_ref, o_ref, w_bf, *, n_sub):
    @pl.when(pl.program_id(0) == 0)
    def _():
        w_bf[...] = w_ref[...].astype(BF16)

    ts = o_ref.shape[0] // n_sub
    acc = [_dot(z_ref[s * ts:(s + 1) * ts, :], w_bf[...]) for s in range(n_sub)]
    for s in range(n_sub):
        rows = slice(s * ts, (s + 1) * ts)
        y = ALPHA * h_ref[rows, :] + acc[s]
        o_ref[rows, :] = _layer_norm(y, g_ref[...], b_ref[...])


def _proj_ln(z, w, w_layer, h, ln_g, ln_b, ln_row, *, tm=512, n_sub=2):
    m, k = z.shape
    d = w.shape[-1]
    return pl.pallas_call(
        functools.partial(_proj_ln_kernel, n_sub=n_sub),
        grid=(m // tm,),
        in_specs=[
            pl.BlockSpec((tm, k), lambda i: (i, 0)),
            pl.BlockSpec((None, k, d), lambda i: (w_layer, 0, 0), pipeline_mode=pl.Buffered(1)),
            pl.BlockSpec((tm, d), lambda i: (i, 0)),
            pl.BlockSpec((None, 1, d), lambda i: (ln_row, 0, 0)),
            pl.BlockSpec((None, 1, d), lambda i: (ln_row, 0, 0)),
        ],
        out_specs=pl.BlockSpec((tm, d), lambda i: (i, 0)),
        out_shape=jax.ShapeDtypeStruct((m, d), F32),
        scratch_shapes=[pltpu.VMEM((k, d), BF16)],
        compiler_params=_params(1),
        name="proj_ln",
    )(z, w, h, ln_g, ln_b)


def _mlp_ln_kernel(h_ref, w1_ref, w2_ref, g_ref, b_ref, o_ref, xb_ref, *, n_sub, ln_rows):
    k = pl.program_id(1)
    tm = o_ref.shape[0]
    ts = w1_ref.shape[1] // n_sub

    @pl.when(k == 0)
    def _():
        xb_ref[...] = h_ref[...].astype(BF16)
        o_ref[...] = ALPHA * h_ref[...]

    x = xb_ref[...]
    pre = [_dot(x, w1_ref[:, c * ts:(c + 1) * ts].astype(BF16)) for c in range(n_sub)]
    part = None
    for c in range(n_sub):
        a = jnp.maximum(pre[c], 0.0)
        p = _dot((a * a).astype(BF16), w2_ref[c * ts:(c + 1) * ts, :].astype(BF16))
        part = p if part is None else part + p
    o_ref[...] += part

    @pl.when(k == pl.num_programs(1) - 1)
    def _():
        def ln_chunk(r, carry):
            rows = pl.ds(pl.multiple_of(r * ln_rows, ln_rows), ln_rows)
            o_ref[rows, :] = _layer_norm(o_ref[rows, :], g_ref[...], b_ref[...])
            return carry

        lax.fori_loop(0, tm // ln_rows, ln_chunk, 0)


def _mlp_ln(h, w1, w2, layer, ln_g, ln_b, *, tm=1024, tf=512, n_sub=2, ln_rows=256):
    m, d = h.shape
    f = w1.shape[-1]
    ln_row = 2 * layer + 1
    return pl.pallas_call(
        functools.partial(_mlp_ln_kernel, n_sub=n_sub, ln_rows=ln_rows),
        grid=(m // tm, f // tf),
        in_specs=[
            pl.BlockSpec((tm, d), lambda i, k: (i, 0), pipeline_mode=pl.Buffered(1)),
            pl.BlockSpec((None, d, tf), lambda i, k: (layer, 0, k)),
            pl.BlockSpec((None, tf, d), lambda i, k: (layer, k, 0)),
            pl.BlockSpec((None, 1, d), lambda i, k: (ln_row, 0, 0)),
            pl.BlockSpec((None, 1, d), lambda i, k: (ln_row, 0, 0)),
        ],
        out_specs=pl.BlockSpec((tm, d), lambda i, k: (i, 0)),
        out_shape=jax.ShapeDtypeStruct((m, d), F32),
        scratch_shapes=[pltpu.VMEM((tm, d), BF16)],
        compiler_params=_params(2),
        name="mlp_ln",
    )(h, w1, w2, ln_g, ln_b)


def _kv_kernel(h_ref, wd_ref, g_ref, wuk_ref, wuvt_ref, cos_ref, sin_ref, k_ref, v_ref, *, n_sub):
    ts = h_ref.shape[0] // n_sub
    wd = wd_ref[...]
    ckv = [_dot(h_ref[s * ts:(s + 1) * ts, :].astype(BF16), wd)
           for s in range(n_sub)]
    kn, vt = [], []
    for s in range(n_sub):
        c = _rms_norm(ckv[s][:, :KV_LORA], g_ref[...]).astype(BF16)
        kn.append(_dot(c, wuk_ref[...]))
        vt.append(_dot_nt(wuvt_ref[...], c))
    for s in range(n_sub):
        rows = slice(s * ts, (s + 1) * ts)
        pe = (ckv[s][:, KV_LORA:KV_LORA + LANES] * cos_ref[rows, :]
              + ckv[s][:, KV_LORA + LANES:] * sin_ref[rows, :])
        lane = lax.broadcasted_iota(jnp.int32, pe.shape, 1)
        pe_lo = jnp.where(lane < QK_ROPE, pe, 0.0).astype(BF16)
        pe_hi = jnp.where(lane >= QK_ROPE, pe, 0.0).astype(BF16)
        for hd in range(N_HEADS):
            k_ref[0, hd, rows, :QK_NOPE] = kn[s][:, hd * QK_NOPE:(hd + 1) * QK_NOPE].astype(BF16)
            k_ref[0, hd, rows, QK_NOPE:] = pe_lo if hd % 2 == 0 else pe_hi
            v_ref[0, hd, 0, :, rows] = vt[s][hd * V_HEAD:(hd + 1) * V_HEAD, :].astype(BF16)


def _kv_proj(h, wd_ext, g, wuk, wuvt, cos_t, sin_t, *, bsz, seq, n_sub=2):
    m, d = h.shape
    tm = ATTN_TK
    tps = seq // tm
    return pl.pallas_call(
        functools.partial(_kv_kernel, n_sub=n_sub),
        grid=(m // tm,),
        in_specs=[
            pl.BlockSpec((tm, d), lambda i: (i, 0)),
            pl.BlockSpec(wd_ext.shape, lambda i: (0, 0), pipeline_mode=pl.Buffered(1)),
            pl.BlockSpec((1, KV_LORA), lambda i: (0, 0)),
            pl.BlockSpec(wuk.shape, lambda i: (0, 0), pipeline_mode=pl.Buffered(1)),
            pl.BlockSpec(wuvt.shape, lambda i: (0, 0), pipeline_mode=pl.Buffered(1)),
            pl.BlockSpec((tm, LANES), lambda i: (i % tps, 0)),
            pl.BlockSpec((tm, LANES), lambda i: (i % tps, 0)),
        ],
        out_specs=[
            pl.BlockSpec((1, N_HEADS, tm, QK_PAD), lambda i: (i // tps, 0, i % tps, 0)),
            pl.BlockSpec((1, N_HEADS, 1, V_HEAD, tm), lambda i: (i // tps, 0, i % tps, 0, 0)),
        ],
        out_shape=[
            jax.ShapeDtypeStruct((bsz, N_HEADS, seq, QK_PAD), BF16),
            jax.ShapeDtypeStruct((bsz, N_HEADS, tps, V_HEAD, tm), BF16),
        ],
        compiler_params=_params(1),
        name="kv_proj",
    )(h, wd_ext, g, wuk, wuvt, cos_t, sin_t)


def _q_kernel(h_ref, wd_ref, g_ref, wut_ref, cos_ref, sin_ref, q_ref, *, n_sub):
    ts = h_ref.shape[0] // n_sub
    wd = wd_ref[...]
    lat = [_dot(h_ref[s * ts:(s + 1) * ts, :].astype(BF16), wd) for s in range(n_sub)]
    qts = []
    for s in range(n_sub):
        qc = _rms_norm(lat[s], g_ref[...]).astype(BF16)
        qts.append(_dot_nt(wut_ref[...], qc))
    nope_w = N_HEADS * QK_NOPE
    pe_w = N_HEADS * QK_ROPE
    for s in range(n_sub):
        cols = slice(s * ts, (s + 1) * ts)
        qt = qts[s]
        cos = cos_ref[:, cols]
        sin = sin_ref[:, cols]
        for pair in range(N_HEADS // 2):
            lo = nope_w + pair * LANES
            pe = (qt[lo:lo + LANES, :] * cos
                  + qt[lo + pe_w:lo + pe_w + LANES, :] * sin).astype(BF16)
            for hd in (2 * pair, 2 * pair + 1):
                q_ref[0, hd, :QK_NOPE, cols] = qt[hd * QK_NOPE:(hd + 1) * QK_NOPE, :].astype(BF16)
                q_ref[0, hd, QK_NOPE:, cols] = pe


def _q_proj(h, wd, g, wut_ext, cos_tt, sin_tt, *, bsz, seq, tm=512, n_sub=2):
    m, d = h.shape
    tps = seq // tm
    return pl.pallas_call(
        functools.partial(_q_kernel, n_sub=n_sub),
        grid=(m // tm,),
        in_specs=[
            pl.BlockSpec((tm, d), lambda i: (i, 0)),
            pl.BlockSpec(wd.shape, lambda i: (0, 0), pipeline_mode=pl.Buffered(1)),
            pl.BlockSpec((1, Q_LORA), lambda i: (0, 0)),
            pl.BlockSpec(wut_ext.shape, lambda i: (0, 0), pipeline_mode=pl.Buffered(1)),
            pl.BlockSpec((LANES, tm), lambda i: (0, i % tps)),
            pl.BlockSpec((LANES, tm), lambda i: (0, i % tps)),
        ],
        out_specs=pl.BlockSpec((1, N_HEADS, QK_PAD, tm), lambda i: (i // tps, 0, 0, i % tps)),
        out_shape=jax.ShapeDtypeStruct((bsz, N_HEADS, QK_PAD, seq), BF16),
        compiler_params=_params(1),
        name="q_proj",
    )(h, wd, g, wut_ext, cos_tt, sin_tt)


def _attn_kernel(q_ref, k_ref, v_ref, o_ref, m_ref, l_ref, acc_ref, *, tq, tk, c):
    qi = pl.program_id(2)
    cw = MXU_DIM
    n_chain = tq // cw
    kv_per_q = tq // tk

    m_ref[...] = jnp.full(m_ref.shape, -jnp.inf, F32)
    l_ref[...] = jnp.zeros(l_ref.shape, F32)
    acc_ref[...] = jnp.zeros(acc_ref.shape, F32)

    def scores(k, ch):
        return _dot(k, q_ref[0, 0, :, ch * cw:(ch + 1) * cw])

    def chain_step(state, s, vt, mask_shift):
        m_prev, l_prev, acc_prev = state
        if mask_shift is not None:
            row = lax.broadcasted_iota(jnp.int32, s.shape, 0)
            col = lax.broadcasted_iota(jnp.int32, s.shape, 1)
            s = jnp.where(row <= col + mask_shift, s, -jnp.inf)
        m_new = jnp.maximum(m_prev, jnp.max(s, axis=0, keepdims=True))
        p = jnp.exp2(s * c - m_new * c)
        corr = jnp.exp2((m_prev - m_new) * c)
        l_new = corr * l_prev + jnp.sum(p, axis=0, keepdims=True)
        acc_new = corr * acc_prev + _dot(vt, p.astype(BF16))
        return m_new, l_new, acc_new

    def load_state(ch):
        sl = slice(ch * cw, (ch + 1) * cw)
        return m_ref[:, sl], l_ref[:, sl], acc_ref[:, sl]

    def store_state(ch, state):
        sl = slice(ch * cw, (ch + 1) * cw)
        m_ref[:, sl], l_ref[:, sl], acc_ref[:, sl] = state

    def kv_block(j):
        k = k_ref[0, 0, pl.ds(pl.multiple_of(j * tk, tk), tk), :]
        return k, v_ref[0, 0, j]

    def run(first_kv, blocks):
        kv = [kv_block(first_kv + u) for u in range(kv_per_q)]
        s_all = [scores(kv[u][0], ch) for u, ch, _ in blocks]
        states = [load_state(ch) for ch in range(n_chain)]
        for (u, ch, shift), s in zip(blocks, s_all):
            states[ch] = chain_step(states[ch], s, kv[u][1], shift)
        return states

    def body(t, carry):
        states = run(t * kv_per_q,
                     [(u, ch, None) for u in range(kv_per_q) for ch in range(n_chain)])
        for ch in range(n_chain):
            store_state(ch, states[ch])
        return carry

    lax.fori_loop(0, qi, body, 0)

    diag = []
    for u in range(kv_per_q):
        for ch in range(n_chain):
            shift = ch * cw - u * tk
            if shift >= 0:
                diag.append((u, ch, None if shift >= tk - 1 else shift))
    states = run(qi * kv_per_q, diag)

    for ch in range(n_chain):
        _, l_fin, acc_fin = states[ch]
        out = (acc_fin * (1.0 / l_fin)).T
        o_ref[0, ch * cw:(ch + 1) * cw, :] = out.astype(BF16)


def _attention(qt, k, vt):
    bsz, nh, _, seq = qt.shape
    tq, tk = ATTN_TQ, ATTN_TK
    c = (QK_NOPE + QK_ROPE) ** -0.5 * math.log2(math.e)
    kern = functools.partial(_attn_kernel, tq=tq, tk=tk, c=c)
    return pl.pallas_call(
        kern,
        grid=(bsz, nh, seq // tq),
        in_specs=[
            pl.BlockSpec((1, 1, QK_PAD, tq), lambda b, h, i: (b, h, 0, i)),
            pl.BlockSpec((1, 1, seq, QK_PAD), lambda b, h, i: (b, h, 0, 0)),
            pl.BlockSpec((1, 1, seq // tk, V_HEAD, tk), lambda b, h, i: (b, h, 0, 0, 0)),
        ],
        out_specs=pl.BlockSpec((1, tq, V_HEAD), lambda b, h, i: (b, i, h)),
        out_shape=jax.ShapeDtypeStruct((bsz, seq, nh * V_HEAD), BF16),
        scratch_shapes=[
            pltpu.VMEM((1, tq), F32),
            pltpu.VMEM((1, tq), F32),
            pltpu.VMEM((V_HEAD, tq), F32),
        ],
        compiler_params=_params(3),
        name="attention",
    )(qt, k, vt)


def _rotate_half_cols(w):
    half = w.shape[-1] // 2
    return jnp.concatenate([-w[..., half:], w[..., :half]], axis=-1)


def _rope_tables(seq):
    inv = 1.0 / (ROPE_THETA ** (jnp.arange(0, QK_ROPE, 2, dtype=F32) / QK_ROPE))
    ang = jnp.arange(seq, dtype=F32)[:, None] * inv[None, :]
    cos, sin = jnp.cos(ang), jnp.sin(ang)
    return jnp.tile(cos, (1, 4)), jnp.tile(sin, (1, 4))


def kernel(x, ln_g, ln_b, conv_w_in, conv_w, conv_w_out, kv_w_dkv, kv_norm_g, kv_w_ukv,
           mla_w_dq, mla_q_norm_g, mla_w_uq, mla_w_o, mlp_w1, mlp_w2):
    bsz, seq, d = x.shape
    m = bsz * seq
    h = x.reshape(m, d)
    cos_t, sin_t = _rope_tables(seq)
    cos_tt, sin_tt = cos_t.T, sin_t.T
    ln_g = ln_g.reshape(2 * DEPTH, 1, d)
    ln_b = ln_b.reshape(2 * DEPTH, 1, d)

    k_all = vt_all = None
    for layer in range(DEPTH):
        if layer < N_A:
            z = _conv_in(h, conv_w_in, conv_w, layer, seq=seq)
            h = _proj_ln(z, conv_w_out, layer, h, ln_g, ln_b, 2 * layer)
        else:
            if layer == N_A:
                pe_w = kv_w_dkv[:, KV_LORA:]
                rot_w = _rotate_half_cols(pe_w)
                wd_ext = jnp.concatenate(
                    [kv_w_dkv[:, :KV_LORA], pe_w, pe_w, rot_w, rot_w], axis=1).astype(BF16)
                wu = kv_w_ukv.reshape(KV_LORA, N_HEADS, QK_NOPE + V_HEAD)
                wuk = wu[:, :, :QK_NOPE].reshape(KV_LORA, -1).astype(BF16)
                wuvt = wu[:, :, QK_NOPE:].reshape(KV_LORA, -1).T.astype(BF16)
                k_all, vt_all = _kv_proj(h, wd_ext, kv_norm_g[None, :], wuk, wuvt,
                                         cos_t, sin_t, bsz=bsz, seq=seq)
            j = layer - N_A
            wu = mla_w_uq[j].reshape(Q_LORA, N_HEADS, QK_NOPE + QK_ROPE)
            wu_pe = wu[:, :, QK_NOPE:]
            wut_ext = jnp.concatenate(
                [wu[:, :, :QK_NOPE].reshape(Q_LORA, -1), wu_pe.reshape(Q_LORA, -1),
                 _rotate_half_cols(wu_pe).reshape(Q_LORA, -1)], axis=1).T.astype(BF16)
            qt_all = _q_proj(h, mla_w_dq[j].astype(BF16), mla_q_norm_g[j][None, :], wut_ext,
                             cos_tt, sin_tt, bsz=bsz, seq=seq)
            o = _attention(qt_all, k_all, vt_all)
            h = _proj_ln(o.reshape(m, N_HEADS * V_HEAD), mla_w_o, j, h, ln_g, ln_b, 2 * layer)
        h = _mlp_ln(h, mlp_w1, mlp_w2, layer, ln_g, ln_b)
    return h.reshape(bsz, seq, d)
```

```python
import functools
import math

import jax
import jax.numpy as jnp
from jax import lax
from jax.experimental import pallas as pl
from jax.experimental.pallas import tpu as pltpu

D_MODEL = 2048
DEPTH = 4
N_A = DEPTH // 2
D_FF = 4 * D_MODEL
CONV_WIDTH = 3
N_HEADS = 16
QK_NOPE = 128
QK_ROPE = 64
V_HEAD = 128
Q_LORA = 512
KV_LORA = 512
ROPE_THETA = 10000.0
ALPHA = (2 * DEPTH) ** 0.25
LN_EPS = 1e-5
RMS_EPS = 1e-6

LANES = 128
SUBLANES = 8
MXU_DIM = 256
QK_PAD = 2 * LANES
VMEM_LIMIT = 56 * 1024 * 1024

ATTN_TK = 512
ATTN_LOOKAHEAD = 4
V_EXT = V_HEAD + 16

F32 = jnp.float32
BF16 = jnp.bfloat16
NT_DIMS = (((1,), (1,)), ((), ()))


def _params(n_axes):
    return pltpu.CompilerParams(
        dimension_semantics=("arbitrary",) * n_axes, vmem_limit_bytes=VMEM_LIMIT)


def _dot(a, b):
    return jnp.dot(a, b, preferred_element_type=F32)


def _dot_nt(a, b):
    return lax.dot_general(a, b, NT_DIMS, preferred_element_type=F32)


def _layer_norm(y, g, b):
    mu = jnp.mean(y, axis=-1, keepdims=True)
    yc = y - mu
    var = jnp.mean(yc * yc, axis=-1, keepdims=True)
    return yc * lax.rsqrt(var + LN_EPS) * g + b


def _rms_norm(y, g):
    return y * lax.rsqrt(jnp.mean(y * y, axis=-1, keepdims=True) + RMS_EPS) * g


def _conv_in_kernel(h_ref, wb_ref, wc_ref, wu_ref, cw_ref, z_ref, w_bf, vbuf,
                    *, tm, tn, tiles_per_seq, n_sub):
    i = pl.program_id(1)

    @pl.when(i == 0)
    def _():
        w_bf[:, :tn] = wb_ref[...].astype(BF16)
        w_bf[:, tn:2 * tn] = wc_ref[...].astype(BF16)
        w_bf[:, 2 * tn:] = wu_ref[...].astype(BF16)

    @pl.when(i % tiles_per_seq == 0)
    def _():
        vbuf[0:SUBLANES, :] = jnp.zeros((SUBLANES, tn), F32)

    @pl.when(i % tiles_per_seq != 0)
    def _():
        vbuf[0:SUBLANES, :] = vbuf[tm:tm + SUBLANES, :]

    ts = tm // n_sub
    w = w_bf[...]
    bcu = [_dot(h_ref[s * ts:(s + 1) * ts, :].astype(BF16), w) for s in range(n_sub)]
    cw = cw_ref[...]
    for s in range(n_sub):
        r0 = s * ts + SUBLANES
        gate_b = bcu[s][:, :tn]
        v = bcu[s][:, tn:2 * tn] * bcu[s][:, 2 * tn:]
        vbuf[r0:r0 + ts, :] = v
        y = (cw[2:3, :] * v
             + cw[1:2, :] * vbuf[r0 - 1:r0 + ts - 1, :]
             + cw[0:1, :] * vbuf[r0 - 2:r0 + ts - 2, :])
        z_ref[s * ts:(s + 1) * ts, :] = (gate_b * y).astype(BF16)


def _conv_in(h, w_in, conv_w, layer, *, seq, tm=512, tn=512, n_sub=2):
    m, d = h.shape
    nj = d // tn
    kern = functools.partial(_conv_in_kernel, tm=tm, tn=tn, tiles_per_seq=seq // tm, n_sub=n_sub)
    w_spec = [pl.BlockSpec((None, d, tn), lambda j, i, g=g: (layer, 0, g * nj + j))
              for g in range(3)]
    return pl.pallas_call(
        kern,
        grid=(nj, m // tm),
        in_specs=[pl.BlockSpec((tm, d), lambda j, i: (i, 0))] + w_spec + [
            pl.BlockSpec((None, CONV_WIDTH, tn), lambda j, i: (layer, 0, j))],
        out_specs=pl.BlockSpec((tm, tn), lambda j, i: (i, j)),
        out_shape=jax.ShapeDtypeStruct((m, d), BF16),
        scratch_shapes=[pltpu.VMEM((d, 3 * tn), BF16),
                        pltpu.VMEM((tm + SUBLANES, tn), F32)],
        compiler_params=_params(2),
        name="conv_in",
    )(h, w_in, w_in, w_in, conv_w)


def _proj_ln_kernel(z_ref, w_ref, h_ref, g_ref, b_ref, o_ref, w_bf, *, n_sub):
    @pl.when(pl.program_id(0) == 0)
    def _():
        w_bf[...] = w_ref[...].astype(BF16)

    ts = o_ref.shape[0] // n_sub
    acc = [_dot(z_ref[s * ts:(s + 1) * ts, :], w_bf[...]) for s in range(n_sub)]
    for s in range(n_sub):
        rows = slice(s * ts, (s + 1) * ts)
        y = ALPHA * h_ref[rows, :] + acc[s]
        o_ref[rows, :] = _layer_norm(y, g_ref[...], b_ref[...])


def _proj_ln(z, w, w_layer, h, ln_g, ln_b, ln_row, *, tm=512, n_sub=2):
    m, k = z.shape
    d = w.shape[-1]
    return pl.pallas_call(
        functools.partial(_proj_ln_kernel, n_sub=n_sub),
        grid=(m // tm,),
        in_specs=[
            pl.BlockSpec((tm, k), lambda i: (i, 0)),
            pl.BlockSpec((None, k, d), lambda i: (w_layer, 0, 0), pipeline_mode=pl.Buffered(1)),
            pl.BlockSpec((tm, d), lambda i: (i, 0)),
            pl.BlockSpec((None, 1, d), lambda i: (ln_row, 0, 0)),
            pl.BlockSpec((None, 1, d), lambda i: (ln_row, 0, 0)),
        ],
        out_specs=pl.BlockSpec((tm, d), lambda i: (i, 0)),
        out_shape=jax.ShapeDtypeStruct((m, d), F32),
        scratch_shapes=[pltpu.VMEM((k, d), BF16)],
        compiler_params=_params(1),
        name="proj_ln",
    )(z, w, h, ln_g, ln_b)


def _mlp_ln_kernel(h_ref, w1_ref, w2_ref, g_ref, b_ref, o_ref, xb_ref, *, n_sub, ln_rows):
    k = pl.program_id(1)
    tm = o_ref.shape[0]
    ts = w1_ref.shape[1] // n_sub

    @pl.when(k == 0)
    def _():
        xb_ref[...] = h_ref[...].astype(BF16)
        o_ref[...] = ALPHA * h_ref[...]

    x = xb_ref[...]
    pre = [_dot(x, w1_ref[:, c * ts:(c + 1) * ts].astype(BF16)) for c in range(n_sub)]
    part = None
    for c in range(n_sub):
        a = jnp.maximum(pre[c], 0.0)
        p = _dot((a * a).astype(BF16), w2_ref[c * ts:(c + 1) * ts, :].astype(BF16))
        part = p if part is None else part + p
    o_ref[...] += part

    @pl.when(k == pl.num_programs(1) - 1)
    def _():
        def ln_chunk(r, carry):
            rows = pl.ds(pl.multiple_of(r * ln_rows, ln_rows), ln_rows)
            o_ref[rows, :] = _layer_norm(o_ref[rows, :], g_ref[...], b_ref[...])
            return carry

        lax.fori_loop(0, tm // ln_rows, ln_chunk, 0)


def _mlp_ln(h, w1, w2, layer, ln_g, ln_b, *, tm=1024, tf=512, n_sub=2, ln_rows=256):
    m, d = h.shape
    f = w1.shape[-1]
    ln_row = 2 * layer + 1
    return pl.pallas_call(
        functools.partial(_mlp_ln_kernel, n_sub=n_sub, ln_rows=ln_rows),
        grid=(m // tm, f // tf),
        in_specs=[
            pl.BlockSpec((tm, d), lambda i, k: (i, 0), pipeline_mode=pl.Buffered(1)),
            pl.BlockSpec((None, d, tf), lambda i, k: (layer, 0, k)),
            pl.BlockSpec((None, tf, d), lambda i, k: (layer, k, 0)),
            pl.BlockSpec((None, 1, d), lambda i, k: (ln_row, 0, 0)),
            pl.BlockSpec((None, 1, d), lambda i, k: (ln_row, 0, 0)),
        ],
        out_specs=pl.BlockSpec((tm, d), lambda i, k: (i, 0)),
        out_shape=jax.ShapeDtypeStruct((m, d), F32),
        scratch_shapes=[pltpu.VMEM((tm, d), BF16)],
        compiler_params=_params(2),
        name="mlp_ln",
    )(h, w1, w2, ln_g, ln_b)


def _kv_kernel(h_ref, wd_ref, g_ref, wuk_ref, wuvt_ref, cos_ref, sin_ref, k_ref, v_ref, *, n_sub):
    ts = h_ref.shape[0] // n_sub
    wd = wd_ref[...]
    ckv = [_dot(h_ref[s * ts:(s + 1) * ts, :].astype(BF16), wd)
           for s in range(n_sub)]
    kn, vt = [], []
    for s in range(n_sub):
        c = _rms_norm(ckv[s][:, :KV_LORA], g_ref[...]).astype(BF16)
        kn.append(_dot(c, wuk_ref[...]))
        vt.append(_dot_nt(wuvt_ref[...], c))
    for s in range(n_sub):
        rows = slice(s * ts, (s + 1) * ts)
        pe = (ckv[s][:, KV_LORA:KV_LORA + LANES] * cos_ref[rows, :]
              + ckv[s][:, KV_LORA + LANES:] * sin_ref[rows, :])
        lane = lax.broadcasted_iota(jnp.int32, pe.shape, 1)
        pe_lo = jnp.where(lane < QK_ROPE, pe, 0.0).astype(BF16)
        pe_hi = jnp.where(lane >= QK_ROPE, pe, 0.0).astype(BF16)
        for hd in range(N_HEADS):
            k_ref[0, hd, rows, :QK_NOPE] = kn[s][:, hd * QK_NOPE:(hd + 1) * QK_NOPE].astype(BF16)
            k_ref[0, hd, rows, QK_NOPE:] = pe_lo if hd % 2 == 0 else pe_hi
            v_ref[0, hd, 0, :V_HEAD, rows] = vt[s][hd * V_HEAD:(hd + 1) * V_HEAD, :].astype(BF16)
            v_ref[0, hd, 0, V_HEAD:, rows] = jnp.ones((V_EXT - V_HEAD, ts), BF16)


def _kv_proj(h, wd_ext, g, wuk, wuvt, cos_t, sin_t, *, bsz, seq, n_sub=2):
    m, d = h.shape
    tm = ATTN_TK
    tps = seq // tm
    return pl.pallas_call(
        functools.partial(_kv_kernel, n_sub=n_sub),
        grid=(m // tm,),
        in_specs=[
            pl.BlockSpec((tm, d), lambda i: (i, 0)),
            pl.BlockSpec(wd_ext.shape, lambda i: (0, 0), pipeline_mode=pl.Buffered(1)),
            pl.BlockSpec((1, KV_LORA), lambda i: (0, 0)),
            pl.BlockSpec(wuk.shape, lambda i: (0, 0), pipeline_mode=pl.Buffered(1)),
            pl.BlockSpec(wuvt.shape, lambda i: (0, 0), pipeline_mode=pl.Buffered(1)),
            pl.BlockSpec((tm, LANES), lambda i: (i % tps, 0)),
            pl.BlockSpec((tm, LANES), lambda i: (i % tps, 0)),
        ],
        out_specs=[
            pl.BlockSpec((1, N_HEADS, tm, QK_PAD), lambda i: (i // tps, 0, i % tps, 0)),
            pl.BlockSpec((1, N_HEADS, 1, V_EXT, tm), lambda i: (i // tps, 0, i % tps, 0, 0)),
        ],
        out_shape=[
            jax.ShapeDtypeStruct((bsz, N_HEADS, seq, QK_PAD), BF16),
            jax.ShapeDtypeStruct((bsz, N_HEADS, tps, V_EXT, tm), BF16),
        ],
        compiler_params=_params(1),
        name="kv_proj",
    )(h, wd_ext, g, wuk, wuvt, cos_t, sin_t)


def _q_kernel(h_ref, wd_ref, g_ref, wut_ref, cos_ref, sin_ref, q_ref, *, n_sub):
    ts = h_ref.shape[0] // n_sub
    wd = wd_ref[...]
    lat = [_dot(h_ref[s * ts:(s + 1) * ts, :].astype(BF16), wd) for s in range(n_sub)]
    qts = []
    for s in range(n_sub):
        qc = _rms_norm(lat[s], g_ref[...]).astype(BF16)
        qts.append(_dot_nt(wut_ref[...], qc))
    nope_w = N_HEADS * QK_NOPE
    pe_w = N_HEADS * QK_ROPE
    for s in range(n_sub):
        cols = slice(s * ts, (s + 1) * ts)
        qt = qts[s]
        cos = cos_ref[:, cols]
        sin = sin_ref[:, cols]
        for pair in range(N_HEADS // 2):
            lo = nope_w + pair * LANES
            pe = (qt[lo:lo + LANES, :] * cos
                  + qt[lo + pe_w:lo + pe_w + LANES, :] * sin).astype(BF16)
            for hd in (2 * pair, 2 * pair + 1):
                q_ref[0, hd, :QK_NOPE, cols] = qt[hd * QK_NOPE:(hd + 1) * QK_NOPE, :].astype(BF16)
                q_ref[0, hd, QK_NOPE:, cols] = pe


def _q_proj(h, wd, g, wut_ext, cos_tt, sin_tt, *, bsz, seq, tm=512, n_sub=2):
    m, d = h.shape
    tps = seq // tm
    return pl.pallas_call(
        functools.partial(_q_kernel, n_sub=n_sub),
        grid=(m // tm,),
        in_specs=[
            pl.BlockSpec((tm, d), lambda i: (i, 0)),
            pl.BlockSpec(wd.shape, lambda i: (0, 0), pipeline_mode=pl.Buffered(1)),
            pl.BlockSpec((1, Q_LORA), lambda i: (0, 0)),
            pl.BlockSpec(wut_ext.shape, lambda i: (0, 0), pipeline_mode=pl.Buffered(1)),
            pl.BlockSpec((LANES, tm), lambda i: (0, i % tps)),
            pl.BlockSpec((LANES, tm), lambda i: (0, i % tps)),
        ],
        out_specs=pl.BlockSpec((1, N_HEADS, QK_PAD, tm), lambda i: (i // tps, 0, 0, i % tps)),
        out_shape=jax.ShapeDtypeStruct((bsz, N_HEADS, QK_PAD, seq), BF16),
        compiler_params=_params(1),
        name="q_proj",
    )(h, wd, g, wut_ext, cos_tt, sin_tt)


def _attn_blocks(seq, tk, cw):
    blocks = []
    for u in range(seq // tk):
        for ch in range(seq // cw):
            shift = ch * cw - u * tk
            if shift < 0:
                continue
            if shift == 0:
                blocks.append((u * tk, cw, ch, 0))
            else:
                blocks.append((u * tk, tk, ch, None if shift >= tk - 1 else shift))
    return blocks


def _attn_kernel(q_ref, k_ref, v_ref, o_ref, *, tk, c):
    cw = MXU_DIM
    seq = q_ref.shape[-1]
    blocks = _attn_blocks(seq, tk, cw)

    def scores(blk):
        row0, rows, ch, _ = blk
        return _dot(k_ref[0, 0, row0:row0 + rows, :], q_ref[0, 0, :, ch * cw:(ch + 1) * cw])

    def chain_step(state, s, blk):
        row0, rows, _, mask_shift = blk
        if mask_shift is not None:
            row = lax.broadcasted_iota(jnp.int32, s.shape, 0)
            col = lax.broadcasted_iota(jnp.int32, s.shape, 1)
            s = jnp.where(row <= col + mask_shift, s, -jnp.inf)
        vt = v_ref[0, 0, row0 // tk, :, row0 % tk:row0 % tk + rows]
        m_blk = jnp.max(s, axis=0, keepdims=True)
        if state is None:
            m_new = m_blk
            p = jnp.exp2(s * c - m_new * c)
            return m_new, _dot(vt, p.astype(BF16))
        m_prev, acc_prev = state
        m_new = jnp.maximum(m_prev, m_blk)
        p = jnp.exp2(s * c - m_new * c)
        corr = jnp.exp2((m_prev - m_new) * c)
        return m_new, corr * acc_prev + _dot(vt, p.astype(BF16))

    states = [None] * (seq // cw)
    pending = []
    for idx in range(len(blocks) + ATTN_LOOKAHEAD):
        if idx < len(blocks):
            pending.append(scores(blocks[idx]))
        if idx >= ATTN_LOOKAHEAD:
            blk = blocks[idx - ATTN_LOOKAHEAD]
            states[blk[2]] = chain_step(states[blk[2]], pending.pop(0), blk)

    for ch, (_, acc) in enumerate(states):
        out = (acc[:V_HEAD, :] * (1.0 / acc[V_HEAD:V_HEAD + 1, :])).T
        o_ref[0, ch * cw:(ch + 1) * cw, :] = out.astype(BF16)


def _attention(qt, k, vt):
    bsz, nh, _, seq = qt.shape
    tk = ATTN_TK
    c = (QK_NOPE + QK_ROPE) ** -0.5 * math.log2(math.e)
    kern = functools.partial(_attn_kernel, tk=tk, c=c)
    return pl.pallas_call(
        kern,
        grid=(bsz, nh),
        in_specs=[
            pl.BlockSpec((1, 1, QK_PAD, seq), lambda b, h: (b, h, 0, 0)),
            pl.BlockSpec((1, 1, seq, QK_PAD), lambda b, h: (b, h, 0, 0)),
            pl.BlockSpec((1, 1, seq // tk, V_EXT, tk), lambda b, h: (b, h, 0, 0, 0)),
        ],
        out_specs=pl.BlockSpec((1, seq, V_HEAD), lambda b, h: (b, 0, h)),
        out_shape=jax.ShapeDtypeStruct((bsz, seq, nh * V_HEAD), BF16),
        compiler_params=_params(2),
        name="attention",
    )(qt, k, vt)


def _rotate_half_cols(w):
    half = w.shape[-1] // 2
    return jnp.concatenate([-w[..., half:], w[..., :half]], axis=-1)


def _rope_tables(seq):
    inv = 1.0 / (ROPE_THETA ** (jnp.arange(0, QK_ROPE, 2, dtype=F32) / QK_ROPE))
    ang = jnp.arange(seq, dtype=F32)[:, None] * inv[None, :]
    cos, sin = jnp.cos(ang), jnp.sin(ang)
    return jnp.tile(cos, (1, 4)), jnp.tile(sin, (1, 4))


def kernel(x, ln_g, ln_b, conv_w_in, conv_w, conv_w_out, kv_w_dkv, kv_norm_g, kv_w_ukv,
           mla_w_dq, mla_q_norm_g, mla_w_uq, mla_w_o, mlp_w1, mlp_w2):
    bsz, seq, d = x.shape
    m = bsz * seq
    h = x.reshape(m, d)
    cos_t, sin_t = _rope_tables(seq)
    cos_tt, sin_tt = cos_t.T, sin_t.T
    ln_g = ln_g.reshape(2 * DEPTH, 1, d)
    ln_b = ln_b.reshape(2 * DEPTH, 1, d)

    k_all = vt_all = None
    for layer in range(DEPTH):
        if layer < N_A:
            z = _conv_in(h, conv_w_in, conv_w, layer, seq=seq)
            h = _proj_ln(z, conv_w_out, layer, h, ln_g, ln_b, 2 * layer)
        else:
            if layer == N_A:
                pe_w = kv_w_dkv[:, KV_LORA:]
                rot_w = _rotate_half_cols(pe_w)
                wd_ext = jnp.concatenate(
                    [kv_w_dkv[:, :KV_LORA], pe_w, pe_w, rot_w, rot_w], axis=1).astype(BF16)
                wu = kv_w_ukv.reshape(KV_LORA, N_HEADS, QK_NOPE + V_HEAD)
                wuk = wu[:, :, :QK_NOPE].reshape(KV_LORA, -1).astype(BF16)
                wuvt = wu[:, :, QK_NOPE:].reshape(KV_LORA, -1).T.astype(BF16)
                k_all, vt_all = _kv_proj(h, wd_ext, kv_norm_g[None, :], wuk, wuvt,
                                         cos_t, sin_t, bsz=bsz, seq=seq)
            j = layer - N_A
            wu = mla_w_uq[j].reshape(Q_LORA, N_HEADS, QK_NOPE + QK_ROPE)
            wu_pe = wu[:, :, QK_NOPE:]
            wut_ext = jnp.concatenate(
                [wu[:, :, :QK_NOPE].reshape(Q_LORA, -1), wu_pe.reshape(Q_LORA, -1),
                 _rotate_half_cols(wu_pe).reshape(Q_LORA, -1)], axis=1).T.astype(BF16)
            qt_all = _q_proj(h, mla_w_dq[j].astype(BF16), mla_q_norm_g[j][None, :], wut_ext,
                             cos_tt, sin_tt, bsz=bsz, seq=seq)
            o = _attention(qt_all, k_all, vt_all)
            h = _proj_ln(o.reshape(m, N_HEADS * V_HEAD), mla_w_o, j, h, ln_g, ln_b, 2 * layer)
        h = _mlp_ln(h, mlp_w1, mlp_w2, layer, ln_g, ln_b)
    return h.reshape(bsz, seq, d)
```

```python
import functools
import math

import jax
import jax.numpy as jnp
from jax import lax
from jax.experimental import pallas as pl
from jax.experimental.pallas import tpu as pltpu

D_MODEL = 2048
DEPTH = 4
N_A = DEPTH // 2
D_FF = 4 * D_MODEL
CONV_WIDTH = 3
N_HEADS = 16
QK_NOPE = 128
QK_ROPE = 64
V_HEAD = 128
Q_LORA = 512
KV_LORA = 512
ROPE_THETA = 10000.0
ALPHA = (2 * DEPTH) ** 0.25
LN_EPS = 1e-5
RMS_EPS = 1e-6

LANES = 128
SUBLANES = 8
MXU_DIM = 256
QK_PAD = 2 * LANES
VMEM_LIMIT = 56 * 1024 * 1024

ATTN_TK = 512
ATTN_ROWS = 512
ATTN_LOOKAHEAD = 4
V_EXT = V_HEAD + 16

F32 = jnp.float32
BF16 = jnp.bfloat16
NT_DIMS = (((1,), (1,)), ((), ()))


def _params(n_axes):
    return pltpu.CompilerParams(
        dimension_semantics=("arbitrary",) * n_axes, vmem_limit_bytes=VMEM_LIMIT)


def _dot(a, b):
    return jnp.dot(a, b, preferred_element_type=F32)


def _dot_nt(a, b):
    return lax.dot_general(a, b, NT_DIMS, preferred_element_type=F32)


def _layer_norm(y, g, b):
    mu = jnp.mean(y, axis=-1, keepdims=True)
    yc = y - mu
    var = jnp.mean(yc * yc, axis=-1, keepdims=True)
    return yc * lax.rsqrt(var + LN_EPS) * g + b


def _rms_norm(y, g):
    return y * lax.rsqrt(jnp.mean(y * y, axis=-1, keepdims=True) + RMS_EPS) * g


def _conv_in_kernel(h_ref, wb_ref, wc_ref, wu_ref, cw_ref, z_ref, w_bf, vbuf,
                    *, tm, tn, tiles_per_seq, n_sub):
    i = pl.program_id(1)

    @pl.when(i == 0)
    def _():
        w_bf[:, :tn] = wb_ref[...].astype(BF16)
        w_bf[:, tn:2 * tn] = wc_ref[...].astype(BF16)
        w_bf[:, 2 * tn:] = wu_ref[...].astype(BF16)

    @pl.when(i % tiles_per_seq == 0)
    def _():
        vbuf[0:SUBLANES, :] = jnp.zeros((SUBLANES, tn), F32)

    @pl.when(i % tiles_per_seq != 0)
    def _():
        vbuf[0:SUBLANES, :] = vbuf[tm:tm + SUBLANES, :]

    ts = tm // n_sub
    w = w_bf[...]
    bcu = [_dot(h_ref[s * ts:(s + 1) * ts, :].astype(BF16), w) for s in range(n_sub)]
    cw = cw_ref[...]
    for s in range(n_sub):
        r0 = s * ts + SUBLANES
        gate_b = bcu[s][:, :tn]
        v = bcu[s][:, tn:2 * tn] * bcu[s][:, 2 * tn:]
        vbuf[r0:r0 + ts, :] = v
        y = (cw[2:3, :] * v
             + cw[1:2, :] * vbuf[r0 - 1:r0 + ts - 1, :]
             + cw[0:1, :] * vbuf[r0 - 2:r0 + ts - 2, :])
        z_ref[s * ts:(s + 1) * ts, :] = (gate_b * y).astype(BF16)


def _conv_in(h, w_in, conv_w, layer, *, seq, tm=512, tn=512, n_sub=2):
    m, d = h.shape
    nj = d // tn
    kern = functools.partial(_conv_in_kernel, tm=tm, tn=tn, tiles_per_seq=seq // tm, n_sub=n_sub)
    w_spec = [pl.BlockSpec((None, d, tn), lambda j, i, g=g: (layer, 0, g * nj + j))
              for g in range(3)]
    return pl.pallas_call(
        kern,
        grid=(nj, m // tm),
        in_specs=[pl.BlockSpec((tm, d), lambda j, i: (i, 0))] + w_spec + [
            pl.BlockSpec((None, CONV_WIDTH, tn), lambda j, i: (layer, 0, j))],
        out_specs=pl.BlockSpec((tm, tn), lambda j, i: (i, j)),
        out_shape=jax.ShapeDtypeStruct((m, d), BF16),
        scratch_shapes=[pltpu.VMEM((d, 3 * tn), BF16),
                        pltpu.VMEM((tm + SUBLANES, tn), F32)],
        compiler_params=_params(2),
        name="conv_in",
    )(h, w_in, w_in, w_in, conv_w)


def _proj_ln_kernel(z_ref, w_ref, h_ref, g_ref, b_ref, o_ref, w_bf, *, n_sub):
    @pl.when(pl.program_id(0) == 0)
    def _():
        w_bf[...] = w_ref[...].astype(BF16)

    ts = o_ref.shape[0] // n_sub
    acc = [_dot(z_ref[s * ts:(s + 1) * ts, :], w_bf[...]) for s in range(n_sub)]
    for s in range(n_sub):
        rows = slice(s * ts, (s + 1) * ts)
        y = ALPHA * h_ref[rows, :] + acc[s]
        o_ref[rows, :] = _layer_norm(y, g_ref[...], b_ref[...])


def _proj_ln(z, w, w_layer, h, ln_g, ln_b, ln_row, *, tm=512, n_sub=2):
    m, k = z.shape
    d = w.shape[-1]
    return pl.pallas_call(
        functools.partial(_proj_ln_kernel, n_sub=n_sub),
        grid=(m // tm,),
        in_specs=[
            pl.BlockSpec((tm, k), lambda i: (i, 0)),
            pl.BlockSpec((None, k, d), lambda i: (w_layer, 0, 0), pipeline_mode=pl.Buffered(1)),
            pl.BlockSpec((tm, d), lambda i: (i, 0)),
            pl.BlockSpec((None, 1, d), lambda i: (ln_row, 0, 0)),
            pl.BlockSpec((None, 1, d), lambda i: (ln_row, 0, 0)),
        ],
        out_specs=pl.BlockSpec((tm, d), lambda i: (i, 0)),
        out_shape=jax.ShapeDtypeStruct((m, d), F32),
        scratch_shapes=[pltpu.VMEM((k, d), BF16)],
        compiler_params=_params(1),
        name="proj_ln",
    )(z, w, h, ln_g, ln_b)


def _mlp_ln_kernel(h_hbm, w1_ref, w2_ref, g_ref, b_ref, o_ref, xb_ref, h_buf, h_sem,
                   *, n_sub, ln_rows):
    i = pl.program_id(0)
    k = pl.program_id(1)
    tm = o_ref.shape[0]
    ts = w1_ref.shape[1] // n_sub

    def h_copy(tile):
        rows = pl.ds(pl.multiple_of(tile * tm, tm), tm)
        return pltpu.make_async_copy(h_hbm.at[rows, :], h_buf, h_sem)

    @pl.when((i == 0) & (k == 0))
    def _():
        h_copy(0).start()

    @pl.when(k == 0)
    def _():
        h_copy(i).wait()
        xb_ref[...] = h_buf[...].astype(BF16)
        o_ref[...] = ALPHA * h_buf[...]

    @pl.when((k == 1) & (i + 1 < pl.num_programs(0)))
    def _():
        h_copy(i + 1).start()

    x = xb_ref[...]
    pre = [_dot(x, w1_ref[:, c * ts:(c + 1) * ts].astype(BF16)) for c in range(n_sub)]
    part = None
    for c in range(n_sub):
        a = jnp.maximum(pre[c], 0.0)
        p = _dot((a * a).astype(BF16), w2_ref[c * ts:(c + 1) * ts, :].astype(BF16))
        part = p if part is None else part + p
    o_ref[...] += part

    @pl.when(k == pl.num_programs(1) - 1)
    def _():
        def ln_chunk(r, carry):
            rows = pl.ds(pl.multiple_of(r * ln_rows, ln_rows), ln_rows)
            o_ref[rows, :] = _layer_norm(o_ref[rows, :], g_ref[...], b_ref[...])
            return carry

        lax.fori_loop(0, tm // ln_rows, ln_chunk, 0)


def _mlp_ln(h, w1, w2, layer, ln_g, ln_b, *, tm=1024, tf=512, n_sub=2, ln_rows=256):
    m, d = h.shape
    f = w1.shape[-1]
    ln_row = 2 * layer + 1
    return pl.pallas_call(
        functools.partial(_mlp_ln_kernel, n_sub=n_sub, ln_rows=ln_rows),
        grid=(m // tm, f // tf),
        in_specs=[
            pl.BlockSpec(memory_space=pl.ANY),
            pl.BlockSpec((None, d, tf), lambda i, k: (layer, 0, k)),
            pl.BlockSpec((None, tf, d), lambda i, k: (layer, k, 0)),
            pl.BlockSpec((None, 1, d), lambda i, k: (ln_row, 0, 0)),
            pl.BlockSpec((None, 1, d), lambda i, k: (ln_row, 0, 0)),
        ],
        out_specs=pl.BlockSpec((tm, d), lambda i, k: (i, 0)),
        out_shape=jax.ShapeDtypeStruct((m, d), F32),
        scratch_shapes=[pltpu.VMEM((tm, d), BF16), pltpu.VMEM((tm, d), F32),
                        pltpu.SemaphoreType.DMA(())],
        compiler_params=_params(2),
        name="mlp_ln",
    )(h, w1, w2, ln_g, ln_b)


def _kv_kernel(h_ref, wd_ref, g_ref, wuk_ref, wuvt_ref, cos_ref, sin_ref, k_ref, v_ref, *, n_sub):
    ts = h_ref.shape[0] // n_sub
    wd = wd_ref[...]
    ckv = [_dot(h_ref[s * ts:(s + 1) * ts, :].astype(BF16), wd)
           for s in range(n_sub)]
    kn, vt = [], []
    for s in range(n_sub):
        c = _rms_norm(ckv[s][:, :KV_LORA], g_ref[...]).astype(BF16)
        kn.append(_dot(c, wuk_ref[...]))
        vt.append(_dot_nt(wuvt_ref[...], c))
    for s in range(n_sub):
        rows = slice(s * ts, (s + 1) * ts)
        pe = (ckv[s][:, KV_LORA:KV_LORA + LANES] * cos_ref[rows, :]
              + ckv[s][:, KV_LORA + LANES:] * sin_ref[rows, :])
        lane = lax.broadcasted_iota(jnp.int32, pe.shape, 1)
        pe_lo = jnp.where(lane < QK_ROPE, pe, 0.0).astype(BF16)
        pe_hi = jnp.where(lane >= QK_ROPE, pe, 0.0).astype(BF16)
        for hd in range(N_HEADS):
            k_ref[0, hd, rows, :QK_NOPE] = kn[s][:, hd * QK_NOPE:(hd + 1) * QK_NOPE].astype(BF16)
            k_ref[0, hd, rows, QK_NOPE:] = pe_lo if hd % 2 == 0 else pe_hi
            v_ref[0, hd, 0, :V_HEAD, rows] = vt[s][hd * V_HEAD:(hd + 1) * V_HEAD, :].astype(BF16)
            v_ref[0, hd, 0, V_HEAD:, rows] = jnp.ones((V_EXT - V_HEAD, ts), BF16)


def _kv_proj(h, wd_ext, g, wuk, wuvt, cos_t, sin_t, *, bsz, seq, n_sub=2):
    m, d = h.shape
    tm = ATTN_TK
    tps = seq // tm
    return pl.pallas_call(
        functools.partial(_kv_kernel, n_sub=n_sub),
        grid=(m // tm,),
        in_specs=[
            pl.BlockSpec((tm, d), lambda i: (i, 0)),
            pl.BlockSpec(wd_ext.shape, lambda i: (0, 0), pipeline_mode=pl.Buffered(1)),
            pl.BlockSpec((1, KV_LORA), lambda i: (0, 0)),
            pl.BlockSpec(wuk.shape, lambda i: (0, 0), pipeline_mode=pl.Buffered(1)),
            pl.BlockSpec(wuvt.shape, lambda i: (0, 0), pipeline_mode=pl.Buffered(1)),
            pl.BlockSpec((tm, LANES), lambda i: (i % tps, 0)),
            pl.BlockSpec((tm, LANES), lambda i: (i % tps, 0)),
        ],
        out_specs=[
            pl.BlockSpec((1, N_HEADS, tm, QK_PAD), lambda i: (i // tps, 0, i % tps, 0)),
            pl.BlockSpec((1, N_HEADS, 1, V_EXT, tm), lambda i: (i // tps, 0, i % tps, 0, 0)),
        ],
        out_shape=[
            jax.ShapeDtypeStruct((bsz, N_HEADS, seq, QK_PAD), BF16),
            jax.ShapeDtypeStruct((bsz, N_HEADS, tps, V_EXT, tm), BF16),
        ],
        compiler_params=_params(1),
        name="kv_proj",
    )(h, wd_ext, g, wuk, wuvt, cos_t, sin_t)


def _q_kernel(h_ref, wd_ref, g_ref, wut_ref, cos_ref, sin_ref, q_ref, *, n_sub):
    ts = h_ref.shape[0] // n_sub
    wd = wd_ref[...]
    lat = [_dot(h_ref[s * ts:(s + 1) * ts, :].astype(BF16), wd) for s in range(n_sub)]
    qts = []
    for s in range(n_sub):
        qc = _rms_norm(lat[s], g_ref[...]).astype(BF16)
        qts.append(_dot_nt(wut_ref[...], qc))
    nope_w = N_HEADS * QK_NOPE
    pe_w = N_HEADS * QK_ROPE
    for s in range(n_sub):
        cols = slice(s * ts, (s + 1) * ts)
        qt = qts[s]
        cos = cos_ref[:, cols]
        sin = sin_ref[:, cols]
        for pair in range(N_HEADS // 2):
            lo = nope_w + pair * LANES
            pe = (qt[lo:lo + LANES, :] * cos
                  + qt[lo + pe_w:lo + pe_w + LANES, :] * sin).astype(BF16)
            for hd in (2 * pair, 2 * pair + 1):
                q_ref[0, hd, :QK_NOPE, cols] = qt[hd * QK_NOPE:(hd + 1) * QK_NOPE, :].astype(BF16)
                q_ref[0, hd, QK_NOPE:, cols] = pe


def _q_proj(h, wd, g, wut_ext, cos_tt, sin_tt, *, bsz, seq, tm=512, n_sub=2):
    m, d = h.shape
    tps = seq // tm
    return pl.pallas_call(
        functools.partial(_q_kernel, n_sub=n_sub),
        grid=(m // tm,),
        in_specs=[
            pl.BlockSpec((tm, d), lambda i: (i, 0)),
            pl.BlockSpec(wd.shape, lambda i: (0, 0), pipeline_mode=pl.Buffered(1)),
            pl.BlockSpec((1, Q_LORA), lambda i: (0, 0)),
            pl.BlockSpec(wut_ext.shape, lambda i: (0, 0), pipeline_mode=pl.Buffered(1)),
            pl.BlockSpec((LANES, tm), lambda i: (0, i % tps)),
            pl.BlockSpec((LANES, tm), lambda i: (0, i % tps)),
        ],
        out_specs=pl.BlockSpec((1, N_HEADS, QK_PAD, tm), lambda i: (i // tps, 0, 0, i % tps)),
        out_shape=jax.ShapeDtypeStruct((bsz, N_HEADS, QK_PAD, seq), BF16),
        compiler_params=_params(1),
        name="q_proj",
    )(h, wd, g, wut_ext, cos_tt, sin_tt)


def _attn_blocks(seq, tb, cw):
    blocks = []
    for u in range(seq // tb):
        for ch in range(seq // cw):
            shift = ch * cw - u * tb
            if shift < 0:
                continue
            if shift == 0:
                blocks.append((u * tb, cw, ch, 0))
            else:
                blocks.append((u * tb, tb, ch, None if shift >= tb - 1 else shift))
    return blocks


def _attn_kernel(q_ref, k_ref, v_ref, o_ref, *, tk, c):
    cw = MXU_DIM
    seq = q_ref.shape[-1]
    blocks = _attn_blocks(seq, ATTN_ROWS, cw)

    def scores(blk):
        row0, rows, ch, _ = blk
        return _dot(k_ref[0, 0, row0:row0 + rows, :], q_ref[0, 0, :, ch * cw:(ch + 1) * cw])

    def chain_step(state, s, blk):
        row0, rows, _, mask_shift = blk
        if mask_shift is not None:
            row = lax.broadcasted_iota(jnp.int32, s.shape, 0)
            col = lax.broadcasted_iota(jnp.int32, s.shape, 1)
            s = jnp.where(row <= col + mask_shift, s, -jnp.inf)
        vt = v_ref[0, 0, row0 // tk, :, row0 % tk:row0 % tk + rows]
        m_blk = jnp.max(s, axis=0, keepdims=True)
        if state is None:
            m_new = m_blk
            p = jnp.exp2(s * c - m_new * c)
            return m_new, _dot(vt, p.astype(BF16))
        m_prev, acc_prev = state
        m_new = jnp.maximum(m_prev, m_blk)
        p = jnp.exp2(s * c - m_new * c)
        corr = jnp.exp2((m_prev - m_new) * c)
        return m_new, corr * acc_prev + _dot(vt, p.astype(BF16))

    states = [None] * (seq // cw)
    pending = []
    for idx in range(len(blocks) + ATTN_LOOKAHEAD):
        if idx < len(blocks):
            pending.append(scores(blocks[idx]))
        if idx >= ATTN_LOOKAHEAD:
            blk = blocks[idx - ATTN_LOOKAHEAD]
            states[blk[2]] = chain_step(states[blk[2]], pending.pop(0), blk)

    for ch, (_, acc) in enumerate(states):
        out = (acc[:V_HEAD, :] * (1.0 / acc[V_HEAD:V_HEAD + 1, :])).T
        o_ref[0, ch * cw:(ch + 1) * cw, :] = out.astype(BF16)


def _attention(qt, k, vt):
    bsz, nh, _, seq = qt.shape
    tk = ATTN_TK
    c = (QK_NOPE + QK_ROPE) ** -0.5 * math.log2(math.e)
    kern = functools.partial(_attn_kernel, tk=tk, c=c)
    return pl.pallas_call(
        kern,
        grid=(bsz, nh),
        in_specs=[
            pl.BlockSpec((1, 1, QK_PAD, seq), lambda b, h: (b, h, 0, 0)),
            pl.BlockSpec((1, 1, seq, QK_PAD), lambda b, h: (b, h, 0, 0)),
            pl.BlockSpec((1, 1, seq // tk, V_EXT, tk), lambda b, h: (b, h, 0, 0, 0)),
        ],
        out_specs=pl.BlockSpec((1, seq, V_HEAD), lambda b, h: (b, 0, h)),
        out_shape=jax.ShapeDtypeStruct((bsz, seq, nh * V_HEAD), BF16),
        compiler_params=_params(2),
        name="attention",
    )(qt, k, vt)


def _rotate_half_cols(w):
    half = w.shape[-1] // 2
    return jnp.concatenate([-w[..., half:], w[..., :half]], axis=-1)


def _rope_tables(seq):
    inv = 1.0 / (ROPE_THETA ** (jnp.arange(0, QK_ROPE, 2, dtype=F32) / QK_ROPE))
    ang = jnp.arange(seq, dtype=F32)[:, None] * inv[None, :]
    cos, sin = jnp.cos(ang), jnp.sin(ang)
    return jnp.tile(cos, (1, 4)), jnp.tile(sin, (1, 4))


def kernel(x, ln_g, ln_b, conv_w_in, conv_w, conv_w_out, kv_w_dkv, kv_norm_g, kv_w_ukv,
           mla_w_dq, mla_q_norm_g, mla_w_uq, mla_w_o, mlp_w1, mlp_w2):
    bsz, seq, d = x.shape
    m = bsz * seq
    h = x.reshape(m, d)
    cos_t, sin_t = _rope_tables(seq)
    cos_tt, sin_tt = cos_t.T, sin_t.T
    ln_g = ln_g.reshape(2 * DEPTH, 1, d)
    ln_b = ln_b.reshape(2 * DEPTH, 1, d)

    k_all = vt_all = None
    for layer in range(DEPTH):
        if layer < N_A:
            z = _conv_in(h, conv_w_in, conv_w, layer, seq=seq)
            h = _proj_ln(z, conv_w_out, layer, h, ln_g, ln_b, 2 * layer)
        else:
            if layer == N_A:
                pe_w = kv_w_dkv[:, KV_LORA:]
                rot_w = _rotate_half_cols(pe_w)
                wd_ext = jnp.concatenate(
                    [kv_w_dkv[:, :KV_LORA], pe_w, pe_w, rot_w, rot_w], axis=1).astype(BF16)
                wu = kv_w_ukv.reshape(KV_LORA, N_HEADS, QK_NOPE + V_HEAD)
                wuk = wu[:, :, :QK_NOPE].reshape(KV_LORA, -1).astype(BF16)
                wuvt = wu[:, :, QK_NOPE:].reshape(KV_LORA, -1).T.astype(BF16)
                k_all, vt_all = _kv_proj(h, wd_ext, kv_norm_g[None, :], wuk, wuvt,
                                         cos_t, sin_t, bsz=bsz, seq=seq)
            j = layer - N_A
            wu = mla_w_uq[j].reshape(Q_LORA, N_HEADS, QK_NOPE + QK_ROPE)
            wu_pe = wu[:, :, QK_NOPE:]
            wut_ext = jnp.concatenate(
                [wu[:, :, :QK_NOPE].reshape(Q_LORA, -1), wu_pe.reshape(Q_LORA, -1),
                 _rotate_half_cols(wu_pe).reshape(Q_LORA, -1)], axis=1).T.astype(BF16)
            qt_all = _q_proj(h, mla_w_dq[j].astype(BF16), mla_q_norm_g[j][None, :], wut_ext,
                             cos_tt, sin_tt, bsz=bsz, seq=seq)
            o = _attention(qt_all, k_all, vt_all)
            h = _proj_ln(o.reshape(m, N_HEADS * V_HEAD), mla_w_o, j, h, ln_g, ln_b, 2 * layer)
        h = _mlp_ln(h, mlp_w1, mlp_w2, layer, ln_g, ln_b)
    return h.reshape(bsz, seq, d)
```

```python
import functools
import math

import jax
import jax.numpy as jnp
from jax import lax
from jax.experimental import pallas as pl
from jax.experimental.pallas import tpu as pltpu

D_MODEL = 2048
DEPTH = 4
N_A = DEPTH // 2
D_FF = 4 * D_MODEL
CONV_WIDTH = 3
N_HEADS = 16
QK_NOPE = 128
QK_ROPE = 64
V_HEAD = 128
Q_LORA = 512
KV_LORA = 512
ROPE_THETA = 10000.0
ALPHA = (2 * DEPTH) ** 0.25
LN_EPS = 1e-5
RMS_EPS = 1e-6

LANES = 128
SUBLANES = 8
MXU_DIM = 256
QK_PAD = 2 * LANES
VMEM_LIMIT = 56 * 1024 * 1024

ATTN_TK = 512
ATTN_ROWS = 512
ATTN_LOOKAHEAD = 4
V_EXT = V_HEAD + 16

F32 = jnp.float32
BF16 = jnp.bfloat16
NT_DIMS = (((1,), (1,)), ((), ()))


def _params(n_axes):
    return pltpu.CompilerParams(
        dimension_semantics=("arbitrary",) * n_axes, vmem_limit_bytes=VMEM_LIMIT)


def _dot(a, b):
    return jnp.dot(a, b, preferred_element_type=F32)


def _dot_nt(a, b):
    return lax.dot_general(a, b, NT_DIMS, preferred_element_type=F32)


def _layer_norm(y, g, b):
    mu = jnp.mean(y, axis=-1, keepdims=True)
    yc = y - mu
    var = jnp.mean(yc * yc, axis=-1, keepdims=True)
    return yc * lax.rsqrt(var + LN_EPS) * g + b


def _rms_norm(y, g):
    return y * lax.rsqrt(jnp.mean(y * y, axis=-1, keepdims=True) + RMS_EPS) * g


def _mlp_cast_specs(layer, n_steps, step_of):
    r1, r2 = D_MODEL // n_steps, D_FF // n_steps
    in_specs = [
        pl.BlockSpec((None, r1, D_FF), lambda *g: (layer, step_of(*g), 0)),
        pl.BlockSpec((None, r2, D_MODEL), lambda *g: (layer, step_of(*g), 0)),
    ]
    out_specs = [
        pl.BlockSpec((r1, D_FF), lambda *g: (step_of(*g), 0)),
        pl.BlockSpec((r2, D_MODEL), lambda *g: (step_of(*g), 0)),
    ]
    out_shape = [jax.ShapeDtypeStruct((D_MODEL, D_FF), BF16),
                 jax.ShapeDtypeStruct((D_FF, D_MODEL), BF16)]
    return in_specs, out_specs, out_shape


def _mlp_cast_step(w1_ref, w2_ref, w1b_ref, w2b_ref):
    w1b_ref[...] = w1_ref[...].astype(BF16)
    w2b_ref[...] = w2_ref[...].astype(BF16)


def _conv_in_kernel(h_ref, wb_ref, wc_ref, wu_ref, cw_ref, w1_ref, w2_ref,
                    z_ref, w1b_ref, w2b_ref, w_bf, vbuf, *, tm, tn, tiles_per_seq, n_sub):
    i = pl.program_id(1)
    _mlp_cast_step(w1_ref, w2_ref, w1b_ref, w2b_ref)

    @pl.when(i == 0)
    def _():
        w_bf[:, :tn] = wb_ref[...].astype(BF16)
        w_bf[:, tn:2 * tn] = wc_ref[...].astype(BF16)
        w_bf[:, 2 * tn:] = wu_ref[...].astype(BF16)

    @pl.when(i % tiles_per_seq == 0)
    def _():
        vbuf[0:SUBLANES, :] = jnp.zeros((SUBLANES, tn), F32)

    @pl.when(i % tiles_per_seq != 0)
    def _():
        vbuf[0:SUBLANES, :] = vbuf[tm:tm + SUBLANES, :]

    ts = tm // n_sub
    w = w_bf[...]
    bcu = [_dot(h_ref[s * ts:(s + 1) * ts, :].astype(BF16), w) for s in range(n_sub)]
    cw = cw_ref[...]
    for s in range(n_sub):
        r0 = s * ts + SUBLANES
        gate_b = bcu[s][:, :tn]
        v = bcu[s][:, tn:2 * tn] * bcu[s][:, 2 * tn:]
        vbuf[r0:r0 + ts, :] = v
        y = (cw[2:3, :] * v
             + cw[1:2, :] * vbuf[r0 - 1:r0 + ts - 1, :]
             + cw[0:1, :] * vbuf[r0 - 2:r0 + ts - 2, :])
        z_ref[s * ts:(s + 1) * ts, :] = (gate_b * y).astype(BF16)


def _conv_in(h, w_in, conv_w, mlp_w1, mlp_w2, layer, *, seq, tm=512, tn=512, n_sub=2):
    m, d = h.shape
    nj, ni = d // tn, m // tm
    kern = functools.partial(_conv_in_kernel, tm=tm, tn=tn, tiles_per_seq=seq // tm, n_sub=n_sub)
    w_spec = [pl.BlockSpec((None, d, tn), lambda j, i, g=g: (layer, 0, g * nj + j))
              for g in range(3)]
    c_in, c_out, c_shape = _mlp_cast_specs(layer, nj * ni, lambda j, i: j * ni + i)
    return pl.pallas_call(
        kern,
        grid=(nj, ni),
        in_specs=[pl.BlockSpec((tm, d), lambda j, i: (i, 0))] + w_spec + [
            pl.BlockSpec((None, CONV_WIDTH, tn), lambda j, i: (layer, 0, j))] + c_in,
        out_specs=[pl.BlockSpec((tm, tn), lambda j, i: (i, j))] + c_out,
        out_shape=[jax.ShapeDtypeStruct((m, d), BF16)] + c_shape,
        scratch_shapes=[pltpu.VMEM((d, 3 * tn), BF16),
                        pltpu.VMEM((tm + SUBLANES, tn), F32)],
        compiler_params=_params(2),
        name="conv_in",
    )(h, w_in, w_in, w_in, conv_w, mlp_w1, mlp_w2)


def _proj_ln_kernel(z_ref, w_ref, h_ref, g_ref, b_ref, o_ref, w_bf, *, n_sub):
    @pl.when(pl.program_id(0) == 0)
    def _():
        w_bf[...] = w_ref[...].astype(BF16)

    ts = o_ref.shape[0] // n_sub
    acc = [_dot(z_ref[s * ts:(s + 1) * ts, :], w_bf[...]) for s in range(n_sub)]
    for s in range(n_sub):
        rows = slice(s * ts, (s + 1) * ts)
        y = ALPHA * h_ref[rows, :] + acc[s]
        o_ref[rows, :] = _layer_norm(y, g_ref[...], b_ref[...])


def _proj_ln(z, w, w_layer, h, ln_g, ln_b, ln_row, *, tm=512, n_sub=2):
    m, k = z.shape
    d = w.shape[-1]
    return pl.pallas_call(
        functools.partial(_proj_ln_kernel, n_sub=n_sub),
        grid=(m // tm,),
        in_specs=[
            pl.BlockSpec((tm, k), lambda i: (i, 0)),
            pl.BlockSpec((None, k, d), lambda i: (w_layer, 0, 0), pipeline_mode=pl.Buffered(1)),
            pl.BlockSpec((tm, d), lambda i: (i, 0)),
            pl.BlockSpec((None, 1, d), lambda i: (ln_row, 0, 0)),
            pl.BlockSpec((None, 1, d), lambda i: (ln_row, 0, 0)),
        ],
        out_specs=pl.BlockSpec((tm, d), lambda i: (i, 0)),
        out_shape=jax.ShapeDtypeStruct((m, d), F32),
        scratch_shapes=[pltpu.VMEM((k, d), BF16)],
        compiler_params=_params(1),
        name="proj_ln",
    )(z, w, h, ln_g, ln_b)


def _mlp_ln_kernel(h_hbm, w1_ref, w2_ref, g_ref, b_ref, o_ref, xb_ref, h_buf, h_sem,
                   *, n_sub, ln_rows):
    i = pl.program_id(0)
    k = pl.program_id(1)
    tm = o_ref.shape[0]
    ts = w1_ref.shape[1] // n_sub

    def h_copy(tile):
        rows = pl.ds(pl.multiple_of(tile * tm, tm), tm)
        return pltpu.make_async_copy(h_hbm.at[rows, :], h_buf, h_sem)

    @pl.when((i == 0) & (k == 0))
    def _():
        h_copy(0).start()

    @pl.when(k == 0)
    def _():
        h_copy(i).wait()
        xb_ref[...] = h_buf[...].astype(BF16)
        o_ref[...] = ALPHA * h_buf[...]

    @pl.when((k == 1) & (i + 1 < pl.num_programs(0)))
    def _():
        h_copy(i + 1).start()

    x = xb_ref[...]
    pre = [_dot(x, w1_ref[:, c * ts:(c + 1) * ts]) for c in range(n_sub)]
    part = None
    for c in range(n_sub):
        a = jnp.maximum(pre[c], 0.0)
        p = _dot((a * a).astype(BF16), w2_ref[c * ts:(c + 1) * ts, :])
        part = p if part is None else part + p
    o_ref[...] += part

    @pl.when(k == pl.num_programs(1) - 1)
    def _():
        def ln_chunk(r, carry):
            rows = pl.ds(pl.multiple_of(r * ln_rows, ln_rows), ln_rows)
            o_ref[rows, :] = _layer_norm(o_ref[rows, :], g_ref[...], b_ref[...])
            return carry

        lax.fori_loop(0, tm // ln_rows, ln_chunk, 0)


def _mlp_ln(h, w1, w2, layer, ln_g, ln_b, *, tm=1024, tf=1024, n_sub=2, ln_rows=256):
    m, d = h.shape
    f = w1.shape[-1]
    ln_row = 2 * layer + 1
    return pl.pallas_call(
        functools.partial(_mlp_ln_kernel, n_sub=n_sub, ln_rows=ln_rows),
        grid=(m // tm, f // tf),
        in_specs=[
            pl.BlockSpec(memory_space=pl.ANY),
            pl.BlockSpec((d, tf), lambda i, k: (0, k)),
            pl.BlockSpec((tf, d), lambda i, k: (k, 0)),
            pl.BlockSpec((None, 1, d), lambda i, k: (ln_row, 0, 0)),
            pl.BlockSpec((None, 1, d), lambda i, k: (ln_row, 0, 0)),
        ],
        out_specs=pl.BlockSpec((tm, d), lambda i, k: (i, 0)),
        out_shape=jax.ShapeDtypeStruct((m, d), F32),
        scratch_shapes=[pltpu.VMEM((tm, d), BF16), pltpu.VMEM((tm, d), F32),
                        pltpu.SemaphoreType.DMA(())],
        compiler_params=_params(2),
        name="mlp_ln",
    )(h, w1, w2, ln_g, ln_b)


def _kv_kernel(h_ref, wd_ref, g_ref, wuk_ref, wuvt_ref, cos_ref, sin_ref, k_ref, v_ref, *, n_sub):
    ts = h_ref.shape[0] // n_sub
    wd = wd_ref[...]
    ckv = [_dot(h_ref[s * ts:(s + 1) * ts, :].astype(BF16), wd)
           for s in range(n_sub)]
    kn, vt = [], []
    for s in range(n_sub):
        c = _rms_norm(ckv[s][:, :KV_LORA], g_ref[...]).astype(BF16)
        kn.append(_dot(c, wuk_ref[...]))
        vt.append(_dot_nt(wuvt_ref[...], c))
    for s in range(n_sub):
        rows = slice(s * ts, (s + 1) * ts)
        pe = (ckv[s][:, KV_LORA:KV_LORA + LANES] * cos_ref[rows, :]
              + ckv[s][:, KV_LORA + LANES:] * sin_ref[rows, :])
        lane = lax.broadcasted_iota(jnp.int32, pe.shape, 1)
        pe_lo = jnp.where(lane < QK_ROPE, pe, 0.0).astype(BF16)
        pe_hi = jnp.where(lane >= QK_ROPE, pe, 0.0).astype(BF16)
        for hd in range(N_HEADS):
            k_ref[0, hd, rows, :QK_NOPE] = kn[s][:, hd * QK_NOPE:(hd + 1) * QK_NOPE].astype(BF16)
            k_ref[0, hd, rows, QK_NOPE:] = pe_lo if hd % 2 == 0 else pe_hi
            v_ref[0, hd, 0, :V_HEAD, rows] = vt[s][hd * V_HEAD:(hd + 1) * V_HEAD, :].astype(BF16)
            v_ref[0, hd, 0, V_HEAD:, rows] = jnp.ones((V_EXT - V_HEAD, ts), BF16)


def _kv_proj(h, wd_ext, g, wuk, wuvt, cos_t, sin_t, *, bsz, seq, n_sub=2):
    m, d = h.shape
    tm = ATTN_TK
    tps = seq // tm
    return pl.pallas_call(
        functools.partial(_kv_kernel, n_sub=n_sub),
        grid=(m // tm,),
        in_specs=[
            pl.BlockSpec((tm, d), lambda i: (i, 0)),
            pl.BlockSpec(wd_ext.shape, lambda i: (0, 0), pipeline_mode=pl.Buffered(1)),
            pl.BlockSpec((1, KV_LORA), lambda i: (0, 0)),
            pl.BlockSpec(wuk.shape, lambda i: (0, 0), pipeline_mode=pl.Buffered(1)),
            pl.BlockSpec(wuvt.shape, lambda i: (0, 0), pipeline_mode=pl.Buffered(1)),
            pl.BlockSpec((tm, LANES), lambda i: (i % tps, 0)),
            pl.BlockSpec((tm, LANES), lambda i: (i % tps, 0)),
        ],
        out_specs=[
            pl.BlockSpec((1, N_HEADS, tm, QK_PAD), lambda i: (i // tps, 0, i % tps, 0)),
            pl.BlockSpec((1, N_HEADS, 1, V_EXT, tm), lambda i: (i // tps, 0, i % tps, 0, 0)),
        ],
        out_shape=[
            jax.ShapeDtypeStruct((bsz, N_HEADS, seq, QK_PAD), BF16),
            jax.ShapeDtypeStruct((bsz, N_HEADS, tps, V_EXT, tm), BF16),
        ],
        compiler_params=_params(1),
        name="kv_proj",
    )(h, wd_ext, g, wuk, wuvt, cos_t, sin_t)


def _q_kernel(h_ref, wd_ref, g_ref, wut_ref, cos_ref, sin_ref, q_ref, *, n_sub):
    ts = h_ref.shape[0] // n_sub
    wd = wd_ref[...]
    lat = [_dot(h_ref[s * ts:(s + 1) * ts, :].astype(BF16), wd) for s in range(n_sub)]
    qts = []
    for s in range(n_sub):
        qc = _rms_norm(lat[s], g_ref[...]).astype(BF16)
        qts.append(_dot_nt(wut_ref[...], qc))
    nope_w = N_HEADS * QK_NOPE
    pe_w = N_HEADS * QK_ROPE
    for s in range(n_sub):
        cols = slice(s * ts, (s + 1) * ts)
        qt = qts[s]
        cos = cos_ref[:, cols]
        sin = sin_ref[:, cols]
        for pair in range(N_HEADS // 2):
            lo = nope_w + pair * LANES
            pe = (qt[lo:lo + LANES, :] * cos
                  + qt[lo + pe_w:lo + pe_w + LANES, :] * sin).astype(BF16)
            for hd in (2 * pair, 2 * pair + 1):
                q_ref[0, hd, :QK_NOPE, cols] = qt[hd * QK_NOPE:(hd + 1) * QK_NOPE, :].astype(BF16)
                q_ref[0, hd, QK_NOPE:, cols] = pe


def _q_proj(h, wd, g, wut_ext, cos_tt, sin_tt, *, bsz, seq, tm=512, n_sub=2):
    m, d = h.shape
    tps = seq // tm
    return pl.pallas_call(
        functools.partial(_q_kernel, n_sub=n_sub),
        grid=(m // tm,),
        in_specs=[
            pl.BlockSpec((tm, d), lambda i: (i, 0)),
            pl.BlockSpec(wd.shape, lambda i: (0, 0), pipeline_mode=pl.Buffered(1)),
            pl.BlockSpec((1, Q_LORA), lambda i: (0, 0)),
            pl.BlockSpec(wut_ext.shape, lambda i: (0, 0), pipeline_mode=pl.Buffered(1)),
            pl.BlockSpec((LANES, tm), lambda i: (0, i % tps)),
            pl.BlockSpec((LANES, tm), lambda i: (0, i % tps)),
        ],
        out_specs=pl.BlockSpec((1, N_HEADS, QK_PAD, tm), lambda i: (i // tps, 0, 0, i % tps)),
        out_shape=jax.ShapeDtypeStruct((bsz, N_HEADS, QK_PAD, seq), BF16),
        compiler_params=_params(1),
        name="q_proj",
    )(h, wd, g, wut_ext, cos_tt, sin_tt)


def _attn_blocks(seq, tb, cw):
    blocks = []
    for u in range(seq // tb):
        for ch in range(seq // cw):
            shift = ch * cw - u * tb
            if shift < 0:
                continue
            if shift == 0:
                blocks.append((u * tb, cw, ch, 0))
            else:
                blocks.append((u * tb, tb, ch, None if shift >= tb - 1 else shift))
    return blocks


def _attn_kernel(q_ref, k_ref, v_ref, w1_ref, w2_ref, o_ref, w1b_ref, w2b_ref, *, tk, c):
    cw = MXU_DIM
    seq = q_ref.shape[-1]
    blocks = _attn_blocks(seq, ATTN_ROWS, cw)
    _mlp_cast_step(w1_ref, w2_ref, w1b_ref, w2b_ref)

    def scores(blk):
        row0, rows, ch, _ = blk
        return _dot(k_ref[0, 0, row0:row0 + rows, :], q_ref[0, 0, :, ch * cw:(ch + 1) * cw])

    def chain_step(state, s, blk):
        row0, rows, _, mask_shift = blk
        if mask_shift is not None:
            row = lax.broadcasted_iota(jnp.int32, s.shape, 0)
            col = lax.broadcasted_iota(jnp.int32, s.shape, 1)
            s = jnp.where(row <= col + mask_shift, s, -jnp.inf)
        vt = v_ref[0, 0, row0 // tk, :, row0 % tk:row0 % tk + rows]
        m_blk = jnp.max(s, axis=0, keepdims=True)
        if state is None:
            m_new = m_blk
            p = jnp.exp2(s * c - m_new * c)
            return m_new, _dot(vt, p.astype(BF16))
        m_prev, acc_prev = state
        m_new = jnp.maximum(m_prev, m_blk)
        p = jnp.exp2(s * c - m_new * c)
        corr = jnp.exp2((m_prev - m_new) * c)
        return m_new, corr * acc_prev + _dot(vt, p.astype(BF16))

    states = [None] * (seq // cw)
    pending = []
    for idx in range(len(blocks) + ATTN_LOOKAHEAD):
        if idx < len(blocks):
            pending.append(scores(blocks[idx]))
        if idx >= ATTN_LOOKAHEAD:
            blk = blocks[idx - ATTN_LOOKAHEAD]
            states[blk[2]] = chain_step(states[blk[2]], pending.pop(0), blk)

    for ch, (_, acc) in enumerate(states):
        out = (acc[:V_HEAD, :] * (1.0 / acc[V_HEAD:V_HEAD + 1, :])).T
        o_ref[0, ch * cw:(ch + 1) * cw, :] = out.astype(BF16)


def _attention(qt, k, vt, mlp_w1, mlp_w2, layer):
    bsz, nh, _, seq = qt.shape
    tk = ATTN_TK
    c = (QK_NOPE + QK_ROPE) ** -0.5 * math.log2(math.e)
    kern = functools.partial(_attn_kernel, tk=tk, c=c)
    c_in, c_out, c_shape = _mlp_cast_specs(layer, bsz * nh, lambda b, h: b * nh + h)
    return pl.pallas_call(
        kern,
        grid=(bsz, nh),
        in_specs=[
            pl.BlockSpec((1, 1, QK_PAD, seq), lambda b, h: (b, h, 0, 0)),
            pl.BlockSpec((1, 1, seq, QK_PAD), lambda b, h: (b, h, 0, 0)),
            pl.BlockSpec((1, 1, seq // tk, V_EXT, tk), lambda b, h: (b, h, 0, 0, 0)),
        ] + c_in,
        out_specs=[pl.BlockSpec((1, seq, V_HEAD), lambda b, h: (b, 0, h))] + c_out,
        out_shape=[jax.ShapeDtypeStruct((bsz, seq, nh * V_HEAD), BF16)] + c_shape,
        compiler_params=_params(2),
        name="attention",
    )(qt, k, vt, mlp_w1, mlp_w2)


def _rotate_half_cols(w):
    half = w.shape[-1] // 2
    return jnp.concatenate([-w[..., half:], w[..., :half]], axis=-1)


def _rope_tables(seq):
    inv = 1.0 / (ROPE_THETA ** (jnp.arange(0, QK_ROPE, 2, dtype=F32) / QK_ROPE))
    ang = jnp.arange(seq, dtype=F32)[:, None] * inv[None, :]
    cos, sin = jnp.cos(ang), jnp.sin(ang)
    return jnp.tile(cos, (1, 4)), jnp.tile(sin, (1, 4))


def kernel(x, ln_g, ln_b, conv_w_in, conv_w, conv_w_out, kv_w_dkv, kv_norm_g, kv_w_ukv,
           mla_w_dq, mla_q_norm_g, mla_w_uq, mla_w_o, mlp_w1, mlp_w2):
    bsz, seq, d = x.shape
    m = bsz * seq
    h = x.reshape(m, d)
    cos_t, sin_t = _rope_tables(seq)
    cos_tt, sin_tt = cos_t.T, sin_t.T
    ln_g = ln_g.reshape(2 * DEPTH, 1, d)
    ln_b = ln_b.reshape(2 * DEPTH, 1, d)

    k_all = vt_all = None
    for layer in range(DEPTH):
        if layer < N_A:
            z, w1_bf, w2_bf = _conv_in(h, conv_w_in, conv_w, mlp_w1, mlp_w2, layer, seq=seq)
            h = _proj_ln(z, conv_w_out, layer, h, ln_g, ln_b, 2 * layer)
        else:
            if layer == N_A:
                pe_w = kv_w_dkv[:, KV_LORA:]
                rot_w = _rotate_half_cols(pe_w)
                wd_ext = jnp.concatenate(
                    [kv_w_dkv[:, :KV_LORA], pe_w, pe_w, rot_w, rot_w], axis=1).astype(BF16)
                wu = kv_w_ukv.reshape(KV_LORA, N_HEADS, QK_NOPE + V_HEAD)
                wuk = wu[:, :, :QK_NOPE].reshape(KV_LORA, -1).astype(BF16)
                wuvt = wu[:, :, QK_NOPE:].reshape(KV_LORA, -1).T.astype(BF16)
                k_all, vt_all = _kv_proj(h, wd_ext, kv_norm_g[None, :], wuk, wuvt,
                                         cos_t, sin_t, bsz=bsz, seq=seq)
            j = layer - N_A
            wu = mla_w_uq[j].reshape(Q_LORA, N_HEADS, QK_NOPE + QK_ROPE)
            wu_pe = wu[:, :, QK_NOPE:]
            wut_ext = jnp.concatenate(
                [wu[:, :, :QK_NOPE].reshape(Q_LORA, -1), wu_pe.reshape(Q_LORA, -1),
                 _rotate_half_cols(wu_pe).reshape(Q_LORA, -1)], axis=1).T.astype(BF16)
            qt_all = _q_proj(h, mla_w_dq[j].astype(BF16), mla_q_norm_g[j][None, :], wut_ext,
                             cos_tt, sin_tt, bsz=bsz, seq=seq)
            o, w1_bf, w2_bf = _attention(qt_all, k_all, vt_all, mlp_w1, mlp_w2, layer)
            h = _proj_ln(o.reshape(m, N_HEADS * V_HEAD), mla_w_o, j, h, ln_g, ln_b, 2 * layer)
        h = _mlp_ln(h, w1_bf, w2_bf, layer, ln_g, ln_b)
    return h.reshape(bsz, seq, d)
```

```python
import functools
import math

import jax
import jax.numpy as jnp
from jax import lax
from jax.experimental import pallas as pl
from jax.experimental.pallas import tpu as pltpu

D_MODEL = 2048
DEPTH = 4
N_A = DEPTH // 2
N_B = DEPTH - N_A
D_FF = 4 * D_MODEL
CONV_WIDTH = 3
N_HEADS = 16
QK_NOPE = 128
QK_ROPE = 64
V_HEAD = 128
Q_LORA = 512
KV_LORA = 512
ROPE_THETA = 10000.0
ALPHA = (2 * DEPTH) ** 0.25
LN_EPS = 1e-5
RMS_EPS = 1e-6

LANES = 128
SUBLANES = 8
MXU_DIM = 256
QK_PAD = 2 * LANES
VMEM_LIMIT = 56 * 1024 * 1024

ATTN_TK = 512
ATTN_ROWS = 512
ATTN_LOOKAHEAD = 4
V_EXT = V_HEAD + 16

F32 = jnp.float32
BF16 = jnp.bfloat16
NT_DIMS = (((1,), (1,)), ((), ()))


def _params(n_axes):
    return pltpu.CompilerParams(
        dimension_semantics=("arbitrary",) * n_axes, vmem_limit_bytes=VMEM_LIMIT)


def _dot(a, b):
    return jnp.dot(a, b, preferred_element_type=F32)


def _dot_nt(a, b):
    return lax.dot_general(a, b, NT_DIMS, preferred_element_type=F32)


def _layer_norm(y, g, b):
    mu = jnp.mean(y, axis=-1, keepdims=True)
    yc = y - mu
    var = jnp.mean(yc * yc, axis=-1, keepdims=True)
    return yc * lax.rsqrt(var + LN_EPS) * g + b


def _rms_norm(y, g):
    return y * lax.rsqrt(jnp.mean(y * y, axis=-1, keepdims=True) + RMS_EPS) * g


N_CAST = 3


def _cast_specs(weights, n_steps, step_of):
    in_specs, out_specs, out_shape = [], [], []
    for w, idx in weights:
        rows, cols = w.shape[1] // n_steps, w.shape[2]
        in_specs.append(pl.BlockSpec((None, rows, cols),
                                     lambda *g, idx=idx: (idx, step_of(*g), 0)))
        out_specs.append(pl.BlockSpec((rows, cols), lambda *g: (step_of(*g), 0)))
        out_shape.append(jax.ShapeDtypeStruct(w.shape[1:], BF16))
    return in_specs, out_specs, out_shape


def _cast_step(src_refs, dst_refs):
    for src, dst in zip(src_refs, dst_refs):
        dst[...] = src[...].astype(BF16)


def _conv_in_kernel(h_ref, wb_ref, wc_ref, wu_ref, cw_ref, *refs, tm, tn, tiles_per_seq, n_sub):
    cast_src, (z_ref, *cast_dst), (w_bf, vbuf) = (
        refs[:N_CAST], refs[N_CAST:2 * N_CAST + 1], refs[2 * N_CAST + 1:])
    i = pl.program_id(1)
    _cast_step(cast_src, cast_dst)

    @pl.when(i == 0)
    def _():
        w_bf[:, :tn] = wb_ref[...].astype(BF16)
        w_bf[:, tn:2 * tn] = wc_ref[...].astype(BF16)
        w_bf[:, 2 * tn:] = wu_ref[...].astype(BF16)

    @pl.when(i % tiles_per_seq == 0)
    def _():
        vbuf[0:SUBLANES, :] = jnp.zeros((SUBLANES, tn), F32)

    @pl.when(i % tiles_per_seq != 0)
    def _():
        vbuf[0:SUBLANES, :] = vbuf[tm:tm + SUBLANES, :]

    ts = tm // n_sub
    w = w_bf[...]
    bcu = [_dot(h_ref[s * ts:(s + 1) * ts, :].astype(BF16), w) for s in range(n_sub)]
    cw = cw_ref[...]
    for s in range(n_sub):
        r0 = s * ts + SUBLANES
        gate_b = bcu[s][:, :tn]
        v = bcu[s][:, tn:2 * tn] * bcu[s][:, 2 * tn:]
        vbuf[r0:r0 + ts, :] = v
        y = (cw[2:3, :] * v
             + cw[1:2, :] * vbuf[r0 - 1:r0 + ts - 1, :]
             + cw[0:1, :] * vbuf[r0 - 2:r0 + ts - 2, :])
        z_ref[s * ts:(s + 1) * ts, :] = (gate_b * y).astype(BF16)


def _conv_in(h, w_in, conv_w, cast_weights, layer, *, seq, tm=512, tn=512, n_sub=2):
    m, d = h.shape
    nj, ni = d // tn, m // tm
    kern = functools.partial(_conv_in_kernel, tm=tm, tn=tn, tiles_per_seq=seq // tm, n_sub=n_sub)
    w_spec = [pl.BlockSpec((None, d, tn), lambda j, i, g=g: (layer, 0, g * nj + j))
              for g in range(3)]
    c_in, c_out, c_shape = _cast_specs(cast_weights, nj * ni, lambda j, i: j * ni + i)
    return pl.pallas_call(
        kern,
        grid=(nj, ni),
        in_specs=[pl.BlockSpec((tm, d), lambda j, i: (i, 0))] + w_spec + [
            pl.BlockSpec((None, CONV_WIDTH, tn), lambda j, i: (layer, 0, j))] + c_in,
        out_specs=[pl.BlockSpec((tm, tn), lambda j, i: (i, j))] + c_out,
        out_shape=[jax.ShapeDtypeStruct((m, d), BF16)] + c_shape,
        scratch_shapes=[pltpu.VMEM((d, 3 * tn), BF16),
                        pltpu.VMEM((tm + SUBLANES, tn), F32)],
        compiler_params=_params(2),
        name="conv_in",
    )(h, w_in, w_in, w_in, conv_w, *[w for w, _ in cast_weights])


def _proj_ln_kernel(z_ref, w_ref, h_ref, g_ref, b_ref, o_ref, *, n_sub):
    ts = o_ref.shape[0] // n_sub
    w = w_ref[...]
    acc = [_dot(z_ref[s * ts:(s + 1) * ts, :], w) for s in range(n_sub)]
    for s in range(n_sub):
        rows = slice(s * ts, (s + 1) * ts)
        y = ALPHA * h_ref[rows, :] + acc[s]
        o_ref[rows, :] = _layer_norm(y, g_ref[...], b_ref[...])


def _proj_ln(z, w, h, ln_g, ln_b, ln_row, *, tm=512, n_sub=2):
    m, k = z.shape
    d = w.shape[-1]
    return pl.pallas_call(
        functools.partial(_proj_ln_kernel, n_sub=n_sub),
        grid=(m // tm,),
        in_specs=[
            pl.BlockSpec((tm, k), lambda i: (i, 0)),
            pl.BlockSpec((k, d), lambda i: (0, 0), pipeline_mode=pl.Buffered(1)),
            pl.BlockSpec((tm, d), lambda i: (i, 0)),
            pl.BlockSpec((None, 1, d), lambda i: (ln_row, 0, 0)),
            pl.BlockSpec((None, 1, d), lambda i: (ln_row, 0, 0)),
        ],
        out_specs=pl.BlockSpec((tm, d), lambda i: (i, 0)),
        out_shape=jax.ShapeDtypeStruct((m, d), F32),
        compiler_params=_params(1),
        name="proj_ln",
    )(z, w, h, ln_g, ln_b)


def _mlp_ln_kernel(h_hbm, w1_ref, w2_ref, g_ref, b_ref, o_ref, xb_ref, h_buf, h_sem,
                   *, n_sub, ln_rows):
    i = pl.program_id(0)
    k = pl.program_id(1)
    tm = o_ref.shape[0]
    ts = w1_ref.shape[1] // n_sub

    def h_copy(tile):
        rows = pl.ds(pl.multiple_of(tile * tm, tm), tm)
        return pltpu.make_async_copy(h_hbm.at[rows, :], h_buf, h_sem)

    @pl.when((i == 0) & (k == 0))
    def _():
        h_copy(0).start()

    @pl.when(k == 0)
    def _():
        h_copy(i).wait()
        xb_ref[...] = h_buf[...].astype(BF16)
        o_ref[...] = ALPHA * h_buf[...]

    @pl.when((k == 1) & (i + 1 < pl.num_programs(0)))
    def _():
        h_copy(i + 1).start()

    x = xb_ref[...]
    pre = [_dot(x, w1_ref[:, c * ts:(c + 1) * ts]) for c in range(n_sub)]
    part = None
    for c in range(n_sub):
        a = jnp.maximum(pre[c], 0.0)
        p = _dot((a * a).astype(BF16), w2_ref[c * ts:(c + 1) * ts, :])
        part = p if part is None else part + p
    o_ref[...] += part

    @pl.when(k == pl.num_programs(1) - 1)
    def _():
        def ln_chunk(r, carry):
            rows = pl.ds(pl.multiple_of(r * ln_rows, ln_rows), ln_rows)
            o_ref[rows, :] = _layer_norm(o_ref[rows, :], g_ref[...], b_ref[...])
            return carry

        lax.fori_loop(0, tm // ln_rows, ln_chunk, 0)


def _mlp_ln(h, w1, w2, layer, ln_g, ln_b, *, tm=1024, tf=1024, n_sub=2, ln_rows=256):
    m, d = h.shape
    f = w1.shape[-1]
    ln_row = 2 * layer + 1
    return pl.pallas_call(
        functools.partial(_mlp_ln_kernel, n_sub=n_sub, ln_rows=ln_rows),
        grid=(m // tm, f // tf),
        in_specs=[
            pl.BlockSpec(memory_space=pl.ANY),
            pl.BlockSpec((d, tf), lambda i, k: (0, k)),
            pl.BlockSpec((tf, d), lambda i, k: (k, 0)),
            pl.BlockSpec((None, 1, d), lambda i, k: (ln_row, 0, 0)),
            pl.BlockSpec((None, 1, d), lambda i, k: (ln_row, 0, 0)),
        ],
        out_specs=pl.BlockSpec((tm, d), lambda i, k: (i, 0)),
        out_shape=jax.ShapeDtypeStruct((m, d), F32),
        scratch_shapes=[pltpu.VMEM((tm, d), BF16), pltpu.VMEM((tm, d), F32),
                        pltpu.SemaphoreType.DMA(())],
        compiler_params=_params(2),
        name="mlp_ln",
    )(h, w1, w2, ln_g, ln_b)


def _kv_kernel(h_ref, wd_ref, g_ref, wuk_ref, wuvt_ref, cos_ref, sin_ref, k_ref, v_ref, *, n_sub):
    ts = h_ref.shape[0] // n_sub
    wd = wd_ref[...]
    ckv = [_dot(h_ref[s * ts:(s + 1) * ts, :].astype(BF16), wd)
           for s in range(n_sub)]
    kn, vt = [], []
    for s in range(n_sub):
        c = _rms_norm(ckv[s][:, :KV_LORA], g_ref[...]).astype(BF16)
        kn.append(_dot(c, wuk_ref[...]))
        vt.append(_dot_nt(wuvt_ref[...], c))
    for s in range(n_sub):
        rows = slice(s * ts, (s + 1) * ts)
        pe = (ckv[s][:, KV_LORA:KV_LORA + LANES] * cos_ref[rows, :]
              + ckv[s][:, KV_LORA + LANES:] * sin_ref[rows, :])
        lane = lax.broadcasted_iota(jnp.int32, pe.shape, 1)
        pe_lo = jnp.where(lane < QK_ROPE, pe, 0.0).astype(BF16)
        pe_hi = jnp.where(lane >= QK_ROPE, pe, 0.0).astype(BF16)
        for hd in range(N_HEADS):
            k_ref[0, hd, rows, :QK_NOPE] = kn[s][:, hd * QK_NOPE:(hd + 1) * QK_NOPE].astype(BF16)
            k_ref[0, hd, rows, QK_NOPE:] = pe_lo if hd % 2 == 0 else pe_hi
            v_ref[0, hd, 0, :V_HEAD, rows] = vt[s][hd * V_HEAD:(hd + 1) * V_HEAD, :].astype(BF16)
            v_ref[0, hd, 0, V_HEAD:, rows] = jnp.ones((V_EXT - V_HEAD, ts), BF16)


def _kv_proj(h, wd_ext, g, wuk, wuvt, cos_t, sin_t, *, bsz, seq, n_sub=2):
    m, d = h.shape
    tm = ATTN_TK
    tps = seq // tm
    return pl.pallas_call(
        functools.partial(_kv_kernel, n_sub=n_sub),
        grid=(m // tm,),
        in_specs=[
            pl.BlockSpec((tm, d), lambda i: (i, 0)),
            pl.BlockSpec(wd_ext.shape, lambda i: (0, 0), pipeline_mode=pl.Buffered(1)),
            pl.BlockSpec((1, KV_LORA), lambda i: (0, 0)),
            pl.BlockSpec(wuk.shape, lambda i: (0, 0), pipeline_mode=pl.Buffered(1)),
            pl.BlockSpec(wuvt.shape, lambda i: (0, 0), pipeline_mode=pl.Buffered(1)),
            pl.BlockSpec((tm, LANES), lambda i: (i % tps, 0)),
            pl.BlockSpec((tm, LANES), lambda i: (i % tps, 0)),
        ],
        out_specs=[
            pl.BlockSpec((1, N_HEADS, tm, QK_PAD), lambda i: (i // tps, 0, i % tps, 0)),
            pl.BlockSpec((1, N_HEADS, 1, V_EXT, tm), lambda i: (i // tps, 0, i % tps, 0, 0)),
        ],
        out_shape=[
            jax.ShapeDtypeStruct((bsz, N_HEADS, seq, QK_PAD), BF16),
            jax.ShapeDtypeStruct((bsz, N_HEADS, tps, V_EXT, tm), BF16),
        ],
        compiler_params=_params(1),
        name="kv_proj",
    )(h, wd_ext, g, wuk, wuvt, cos_t, sin_t)


def _q_kernel(h_ref, wd_ref, g_ref, wut_ref, cos_ref, sin_ref, q_ref, *, n_sub):
    ts = h_ref.shape[0] // n_sub
    wd = wd_ref[...]
    lat = [_dot(h_ref[s * ts:(s + 1) * ts, :].astype(BF16), wd) for s in range(n_sub)]
    qts = []
    for s in range(n_sub):
        qc = _rms_norm(lat[s], g_ref[...]).astype(BF16)
        qts.append(_dot_nt(wut_ref[...], qc))
    nope_w = N_HEADS * QK_NOPE
    pe_w = N_HEADS * QK_ROPE
    for s in range(n_sub):
        cols = slice(s * ts, (s + 1) * ts)
        qt = qts[s]
        cos = cos_ref[:, cols]
        sin = sin_ref[:, cols]
        for pair in range(N_HEADS // 2):
            lo = nope_w + pair * LANES
            pe = (qt[lo:lo + LANES, :] * cos
                  + qt[lo + pe_w:lo + pe_w + LANES, :] * sin).astype(BF16)
            for hd in (2 * pair, 2 * pair + 1):
                q_ref[0, hd, :QK_NOPE, cols] = qt[hd * QK_NOPE:(hd + 1) * QK_NOPE, :].astype(BF16)
                q_ref[0, hd, QK_NOPE:, cols] = pe


def _q_proj(h, wd, g, wut_ext, j, cos_tt, sin_tt, *, bsz, seq, tm=512, n_sub=2):
    m, d = h.shape
    tps = seq // tm
    return pl.pallas_call(
        functools.partial(_q_kernel, n_sub=n_sub),
        grid=(m // tm,),
        in_specs=[
            pl.BlockSpec((tm, d), lambda i: (i, 0)),
            pl.BlockSpec((None,) + wd.shape[1:], lambda i: (j, 0, 0),
                         pipeline_mode=pl.Buffered(1)),
            pl.BlockSpec((None, 1, Q_LORA), lambda i: (j, 0, 0)),
            pl.BlockSpec((None,) + wut_ext.shape[1:], lambda i: (j, 0, 0),
                         pipeline_mode=pl.Buffered(1)),
            pl.BlockSpec((LANES, tm), lambda i: (0, i % tps)),
            pl.BlockSpec((LANES, tm), lambda i: (0, i % tps)),
        ],
        out_specs=pl.BlockSpec((1, N_HEADS, QK_PAD, tm), lambda i: (i // tps, 0, 0, i % tps)),
        out_shape=jax.ShapeDtypeStruct((bsz, N_HEADS, QK_PAD, seq), BF16),
        compiler_params=_params(1),
        name="q_proj",
    )(h, wd, g, wut_ext, cos_tt, sin_tt)


def _attn_blocks(seq, tb, cw):
    blocks = []
    for u in range(seq // tb):
        for ch in range(seq // cw):
            shift = ch * cw - u * tb
            if shift < 0:
                continue
            if shift == 0:
                blocks.append((u * tb, cw, ch, 0))
            else:
                blocks.append((u * tb, tb, ch, None if shift >= tb - 1 else shift))
    return blocks


def _attn_kernel(q_ref, k_ref, v_ref, *refs, tk, c):
    cast_src, (o_ref, *cast_dst) = refs[:N_CAST], refs[N_CAST:]
    cw = MXU_DIM
    seq = q_ref.shape[-1]
    blocks = _attn_blocks(seq, ATTN_ROWS, cw)
    _cast_step(cast_src, cast_dst)

    def scores(blk):
        row0, rows, ch, _ = blk
        return _dot(k_ref[0, 0, row0:row0 + rows, :], q_ref[0, 0, :, ch * cw:(ch + 1) * cw])

    def chain_step(state, s, blk):
        row0, rows, _, mask_shift = blk
        if mask_shift is not None:
            row = lax.broadcasted_iota(jnp.int32, s.shape, 0)
            col = lax.broadcasted_iota(jnp.int32, s.shape, 1)
            s = jnp.where(row <= col + mask_shift, s, -jnp.inf)
        vt = v_ref[0, 0, row0 // tk, :, row0 % tk:row0 % tk + rows]
        m_blk = jnp.max(s, axis=0, keepdims=True)
        if state is None:
            m_new = m_blk
            p = jnp.exp2(s * c - m_new * c)
            return m_new, _dot(vt, p.astype(BF16))
        m_prev, acc_prev = state
        m_new = jnp.maximum(m_prev, m_blk)
        p = jnp.exp2(s * c - m_new * c)
        corr = jnp.exp2((m_prev - m_new) * c)
        return m_new, corr * acc_prev + _dot(vt, p.astype(BF16))

    states = [None] * (seq // cw)
    pending = []
    for idx in range(len(blocks) + ATTN_LOOKAHEAD):
        if idx < len(blocks):
            pending.append(scores(blocks[idx]))
        if idx >= ATTN_LOOKAHEAD:
            blk = blocks[idx - ATTN_LOOKAHEAD]
            states[blk[2]] = chain_step(states[blk[2]], pending.pop(0), blk)

    for ch, (_, acc) in enumerate(states):
        out = (acc[:V_HEAD, :] * (1.0 / acc[V_HEAD:V_HEAD + 1, :])).T
        o_ref[0, ch * cw:(ch + 1) * cw, :] = out.astype(BF16)


def _attention(qt, k, vt, cast_weights):
    bsz, nh, _, seq = qt.shape
    tk = ATTN_TK
    c = (QK_NOPE + QK_ROPE) ** -0.5 * math.log2(math.e)
    kern = functools.partial(_attn_kernel, tk=tk, c=c)
    c_in, c_out, c_shape = _cast_specs(cast_weights, bsz * nh, lambda b, h: b * nh + h)
    return pl.pallas_call(
        kern,
        grid=(bsz, nh),
        in_specs=[
            pl.BlockSpec((1, 1, QK_PAD, seq), lambda b, h: (b, h, 0, 0)),
            pl.BlockSpec((1, 1, seq, QK_PAD), lambda b, h: (b, h, 0, 0)),
            pl.BlockSpec((1, 1, seq // tk, V_EXT, tk), lambda b, h: (b, h, 0, 0, 0)),
        ] + c_in,
        out_specs=[pl.BlockSpec((1, seq, V_HEAD), lambda b, h: (b, 0, h))] + c_out,
        out_shape=[jax.ShapeDtypeStruct((bsz, seq, nh * V_HEAD), BF16)] + c_shape,
        compiler_params=_params(2),
        name="attention",
    )(qt, k, vt, *[w for w, _ in cast_weights])


def _rotate_half_cols(w):
    half = w.shape[-1] // 2
    return jnp.concatenate([-w[..., half:], w[..., :half]], axis=-1)


def _rope_tables(seq):
    inv = 1.0 / (ROPE_THETA ** (jnp.arange(0, QK_ROPE, 2, dtype=F32) / QK_ROPE))
    ang = jnp.arange(seq, dtype=F32)[:, None] * inv[None, :]
    cos, sin = jnp.cos(ang), jnp.sin(ang)
    return jnp.tile(cos, (1, 4)), jnp.tile(sin, (1, 4))


def kernel(x, ln_g, ln_b, conv_w_in, conv_w, conv_w_out, kv_w_dkv, kv_norm_g, kv_w_ukv,
           mla_w_dq, mla_q_norm_g, mla_w_uq, mla_w_o, mlp_w1, mlp_w2):
    bsz, seq, d = x.shape
    m = bsz * seq
    h = x.reshape(m, d)
    cos_t, sin_t = _rope_tables(seq)
    cos_tt, sin_tt = cos_t.T, sin_t.T
    ln_g = ln_g.reshape(2 * DEPTH, 1, d)
    ln_b = ln_b.reshape(2 * DEPTH, 1, d)

    uq = mla_w_uq.astype(BF16).reshape(N_B, Q_LORA, N_HEADS, QK_NOPE + QK_ROPE)
    uq_pe = uq[..., QK_NOPE:]
    wut_ext = jnp.concatenate(
        [uq[..., :QK_NOPE].reshape(N_B, Q_LORA, -1), uq_pe.reshape(N_B, Q_LORA, -1),
         _rotate_half_cols(uq_pe).reshape(N_B, Q_LORA, -1)], axis=2).transpose(0, 2, 1)
    wdq = mla_w_dq.astype(BF16)
    q_norm_g = mla_q_norm_g.reshape(N_B, 1, Q_LORA)

    k_all = vt_all = None
    for layer in range(DEPTH):
        if layer < N_A:
            cast = [(conv_w_out, layer), (mlp_w1, layer), (mlp_w2, layer)]
            z, wo_bf, w1_bf, w2_bf = _conv_in(h, conv_w_in, conv_w, cast, layer, seq=seq)
            h = _proj_ln(z, wo_bf, h, ln_g, ln_b, 2 * layer)
        else:
            if layer == N_A:
                dkv = kv_w_dkv.astype(BF16)
                pe_w = dkv[:, KV_LORA:]
                rot_w = _rotate_half_cols(pe_w)
                wd_ext = jnp.concatenate([dkv[:, :KV_LORA], pe_w, pe_w, rot_w, rot_w], axis=1)
                wu = kv_w_ukv.astype(BF16).reshape(KV_LORA, N_HEADS, QK_NOPE + V_HEAD)
                wuk = wu[:, :, :QK_NOPE].reshape(KV_LORA, -1)
                wuvt = wu[:, :, QK_NOPE:].reshape(KV_LORA, -1).T
                k_all, vt_all = _kv_proj(h, wd_ext, kv_norm_g[None, :], wuk, wuvt,
                                         cos_t, sin_t, bsz=bsz, seq=seq)
            j = layer - N_A
            qt_all = _q_proj(h, wdq, q_norm_g, wut_ext, j, cos_tt, sin_tt, bsz=bsz, seq=seq)
            cast = [(mla_w_o, j), (mlp_w1, layer), (mlp_w2, layer)]
            o, wo_bf, w1_bf, w2_bf = _attention(qt_all, k_all, vt_all, cast)
            h = _proj_ln(o.reshape(m, N_HEADS * V_HEAD), wo_bf, h, ln_g, ln_b, 2 * layer)
        h = _mlp_ln(h, w1_bf, w2_bf, layer, ln_g, ln_b)
    return h.reshape(bsz, seq, d)
```

```python
import functools
import math

import jax
import jax.numpy as jnp
from jax import lax
from jax.experimental import pallas as pl
from jax.experimental.pallas import tpu as pltpu

D_MODEL = 2048
DEPTH = 4
N_A = DEPTH // 2
N_B = DEPTH - N_A
D_FF = 4 * D_MODEL
CONV_WIDTH = 3
N_HEADS = 16
QK_NOPE = 128
QK_ROPE = 64
V_HEAD = 128
Q_LORA = 512
KV_LORA = 512
ROPE_THETA = 10000.0
ALPHA = (2 * DEPTH) ** 0.25
LN_EPS = 1e-5
RMS_EPS = 1e-6

LANES = 128
SUBLANES = 8
MXU_DIM = 256
QK_PAD = 2 * LANES
VMEM_LIMIT = 56 * 1024 * 1024

ATTN_TK = 512
ATTN_ROWS = 512
ATTN_LOOKAHEAD = 3
ATTN_HEADS = 2
V_EXT = V_HEAD + 16

F32 = jnp.float32
BF16 = jnp.bfloat16
NT_DIMS = (((1,), (1,)), ((), ()))


def _params(n_axes):
    return pltpu.CompilerParams(
        dimension_semantics=("arbitrary",) * n_axes, vmem_limit_bytes=VMEM_LIMIT)


def _dot(a, b):
    return jnp.dot(a, b, preferred_element_type=F32)


def _dot_nt(a, b):
    return lax.dot_general(a, b, NT_DIMS, preferred_element_type=F32)


def _layer_norm(y, g, b):
    mu = jnp.mean(y, axis=-1, keepdims=True)
    yc = y - mu
    var = jnp.mean(yc * yc, axis=-1, keepdims=True)
    return yc * lax.rsqrt(var + LN_EPS) * g + b


def _rms_norm(y, g):
    return y * lax.rsqrt(jnp.mean(y * y, axis=-1, keepdims=True) + RMS_EPS) * g


N_CAST = 3


def _cast_specs(weights, n_steps, step_of):
    in_specs, out_specs, out_shape = [], [], []
    for w, idx in weights:
        rows, cols = w.shape[1] // n_steps, w.shape[2]
        in_specs.append(pl.BlockSpec((None, rows, cols),
                                     lambda *g, idx=idx: (idx, step_of(*g), 0)))
        out_specs.append(pl.BlockSpec((rows, cols), lambda *g: (step_of(*g), 0)))
        out_shape.append(jax.ShapeDtypeStruct(w.shape[1:], BF16))
    return in_specs, out_specs, out_shape


def _cast_step(src_refs, dst_refs):
    for src, dst in zip(src_refs, dst_refs):
        dst[...] = src[...].astype(BF16)


def _conv_in_kernel(h_ref, wb_ref, wc_ref, wu_ref, cw_ref, *refs, tm, tn, tiles_per_seq, n_sub):
    cast_src, (z_ref, *cast_dst), (w_bf, vbuf) = (
        refs[:N_CAST], refs[N_CAST:2 * N_CAST + 1], refs[2 * N_CAST + 1:])
    i = pl.program_id(1)
    _cast_step(cast_src, cast_dst)

    @pl.when(i == 0)
    def _():
        w_bf[:, :tn] = wb_ref[...].astype(BF16)
        w_bf[:, tn:2 * tn] = wc_ref[...].astype(BF16)
        w_bf[:, 2 * tn:] = wu_ref[...].astype(BF16)

    @pl.when(i % tiles_per_seq == 0)
    def _():
        vbuf[0:SUBLANES, :] = jnp.zeros((SUBLANES, tn), F32)

    @pl.when(i % tiles_per_seq != 0)
    def _():
        vbuf[0:SUBLANES, :] = vbuf[tm:tm + SUBLANES, :]

    ts = tm // n_sub
    w = w_bf[...]
    bcu = [_dot(h_ref[s * ts:(s + 1) * ts, :].astype(BF16), w) for s in range(n_sub)]
    cw = cw_ref[...]
    for s in range(n_sub):
        r0 = s * ts + SUBLANES
        gate_b = bcu[s][:, :tn]
        v = bcu[s][:, tn:2 * tn] * bcu[s][:, 2 * tn:]
        vbuf[r0:r0 + ts, :] = v
        y = (cw[2:3, :] * v
             + cw[1:2, :] * vbuf[r0 - 1:r0 + ts - 1, :]
             + cw[0:1, :] * vbuf[r0 - 2:r0 + ts - 2, :])
        z_ref[s * ts:(s + 1) * ts, :] = (gate_b * y).astype(BF16)


def _conv_in(h, w_in, conv_w, cast_weights, layer, *, seq, tm=512, tn=512, n_sub=2):
    m, d = h.shape
    nj, ni = d // tn, m // tm
    kern = functools.partial(_conv_in_kernel, tm=tm, tn=tn, tiles_per_seq=seq // tm, n_sub=n_sub)
    w_spec = [pl.BlockSpec((None, d, tn), lambda j, i, g=g: (layer, 0, g * nj + j))
              for g in range(3)]
    c_in, c_out, c_shape = _cast_specs(cast_weights, nj * ni, lambda j, i: j * ni + i)
    return pl.pallas_call(
        kern,
        grid=(nj, ni),
        in_specs=[pl.BlockSpec((tm, d), lambda j, i: (i, 0))] + w_spec + [
            pl.BlockSpec((None, CONV_WIDTH, tn), lambda j, i: (layer, 0, j))] + c_in,
        out_specs=[pl.BlockSpec((tm, tn), lambda j, i: (i, j))] + c_out,
        out_shape=[jax.ShapeDtypeStruct((m, d), BF16)] + c_shape,
        scratch_shapes=[pltpu.VMEM((d, 3 * tn), BF16),
                        pltpu.VMEM((tm + SUBLANES, tn), F32)],
        compiler_params=_params(2),
        name="conv_in",
    )(h, w_in, w_in, w_in, conv_w, *[w for w, _ in cast_weights])


def _proj_ln_kernel(z_ref, w_ref, h_ref, g_ref, b_ref, o_ref, *, n_sub):
    ts = o_ref.shape[0] // n_sub
    w = w_ref[...]
    acc = [_dot(z_ref[s * ts:(s + 1) * ts, :], w) for s in range(n_sub)]
    for s in range(n_sub):
        rows = slice(s * ts, (s + 1) * ts)
        y = ALPHA * h_ref[rows, :] + acc[s]
        o_ref[rows, :] = _layer_norm(y, g_ref[...], b_ref[...])


def _proj_ln(z, w, h, ln_g, ln_b, ln_row, *, tm=512, n_sub=2):
    m, k = z.shape
    d = w.shape[-1]
    return pl.pallas_call(
        functools.partial(_proj_ln_kernel, n_sub=n_sub),
        grid=(m // tm,),
        in_specs=[
            pl.BlockSpec((tm, k), lambda i: (i, 0)),
            pl.BlockSpec((k, d), lambda i: (0, 0), pipeline_mode=pl.Buffered(1)),
            pl.BlockSpec((tm, d), lambda i: (i, 0)),
            pl.BlockSpec((None, 1, d), lambda i: (ln_row, 0, 0)),
            pl.BlockSpec((None, 1, d), lambda i: (ln_row, 0, 0)),
        ],
        out_specs=pl.BlockSpec((tm, d), lambda i: (i, 0)),
        out_shape=jax.ShapeDtypeStruct((m, d), F32),
        compiler_params=_params(1),
        name="proj_ln",
    )(z, w, h, ln_g, ln_b)


def _mlp_ln_kernel(h_hbm, w1_ref, w2_ref, g_ref, b_ref, o_ref, xb_ref, h_buf, h_sem,
                   *, n_sub, ln_rows):
    i = pl.program_id(0)
    k = pl.program_id(1)
    tm = o_ref.shape[0]
    ts = w1_ref.shape[1] // n_sub

    def h_copy(tile):
        rows = pl.ds(pl.multiple_of(tile * tm, tm), tm)
        return pltpu.make_async_copy(h_hbm.at[rows, :], h_buf, h_sem)

    @pl.when((i == 0) & (k == 0))
    def _():
        h_copy(0).start()

    @pl.when(k == 0)
    def _():
        h_copy(i).wait()
        xb_ref[...] = h_buf[...].astype(BF16)
        o_ref[...] = ALPHA * h_buf[...]

    @pl.when((k == 1) & (i + 1 < pl.num_programs(0)))
    def _():
        h_copy(i + 1).start()

    x = xb_ref[...]
    pre = [_dot(x, w1_ref[:, c * ts:(c + 1) * ts]) for c in range(n_sub)]
    part = None
    for c in range(n_sub):
        a = jnp.maximum(pre[c], 0.0)
        p = _dot((a * a).astype(BF16), w2_ref[c * ts:(c + 1) * ts, :])
        part = p if part is None else part + p
    o_ref[...] += part

    @pl.when(k == pl.num_programs(1) - 1)
    def _():
        def ln_chunk(r, carry):
            rows = pl.ds(pl.multiple_of(r * ln_rows, ln_rows), ln_rows)
            o_ref[rows, :] = _layer_norm(o_ref[rows, :], g_ref[...], b_ref[...])
            return carry

        lax.fori_loop(0, tm // ln_rows, ln_chunk, 0)


def _mlp_ln(h, w1, w2, layer, ln_g, ln_b, *, tm=1024, tf=1024, n_sub=2, ln_rows=256):
    m, d = h.shape
    f = w1.shape[-1]
    ln_row = 2 * layer + 1
    return pl.pallas_call(
        functools.partial(_mlp_ln_kernel, n_sub=n_sub, ln_rows=ln_rows),
        grid=(m // tm, f // tf),
        in_specs=[
            pl.BlockSpec(memory_space=pl.ANY),
            pl.BlockSpec((d, tf), lambda i, k: (0, k)),
            pl.BlockSpec((tf, d), lambda i, k: (k, 0)),
            pl.BlockSpec((None, 1, d), lambda i, k: (ln_row, 0, 0)),
            pl.BlockSpec((None, 1, d), lambda i, k: (ln_row, 0, 0)),
        ],
        out_specs=pl.BlockSpec((tm, d), lambda i, k: (i, 0)),
        out_shape=jax.ShapeDtypeStruct((m, d), F32),
        scratch_shapes=[pltpu.VMEM((tm, d), BF16), pltpu.VMEM((tm, d), F32),
                        pltpu.SemaphoreType.DMA(())],
        compiler_params=_params(2),
        name="mlp_ln",
    )(h, w1, w2, ln_g, ln_b)


def _kv_kernel(h_ref, wd_ref, g_ref, wuk_ref, wuvt_ref, cos_ref, sin_ref, k_ref, v_ref, *, n_sub):
    ts = h_ref.shape[0] // n_sub
    wd = wd_ref[...]
    ckv = [_dot(h_ref[s * ts:(s + 1) * ts, :].astype(BF16), wd)
           for s in range(n_sub)]
    kn, vt = [], []
    for s in range(n_sub):
        c = _rms_norm(ckv[s][:, :KV_LORA], g_ref[...]).astype(BF16)
        kn.append(_dot(c, wuk_ref[...]))
        vt.append(_dot_nt(wuvt_ref[...], c))
    for s in range(n_sub):
        rows = slice(s * ts, (s + 1) * ts)
        pe = (ckv[s][:, KV_LORA:KV_LORA + LANES] * cos_ref[rows, :]
              + ckv[s][:, KV_LORA + LANES:] * sin_ref[rows, :])
        lane = lax.broadcasted_iota(jnp.int32, pe.shape, 1)
        pe_lo = jnp.where(lane < QK_ROPE, pe, 0.0).astype(BF16)
        pe_hi = jnp.where(lane >= QK_ROPE, pe, 0.0).astype(BF16)
        for hd in range(N_HEADS):
            k_ref[0, hd, rows, :QK_NOPE] = kn[s][:, hd * QK_NOPE:(hd + 1) * QK_NOPE].astype(BF16)
            k_ref[0, hd, rows, QK_NOPE:] = pe_lo if hd % 2 == 0 else pe_hi
            v_ref[0, hd, 0, :V_HEAD, rows] = vt[s][hd * V_HEAD:(hd + 1) * V_HEAD, :].astype(BF16)
            v_ref[0, hd, 0, V_HEAD:, rows] = jnp.ones((V_EXT - V_HEAD, ts), BF16)


def _kv_proj(h, wd_ext, g, wuk, wuvt, cos_t, sin_t, *, bsz, seq, n_sub=2):
    m, d = h.shape
    tm = ATTN_TK
    tps = seq // tm
    return pl.pallas_call(
        functools.partial(_kv_kernel, n_sub=n_sub),
        grid=(m // tm,),
        in_specs=[
            pl.BlockSpec((tm, d), lambda i: (i, 0)),
            pl.BlockSpec(wd_ext.shape, lambda i: (0, 0), pipeline_mode=pl.Buffered(1)),
            pl.BlockSpec((1, KV_LORA), lambda i: (0, 0)),
            pl.BlockSpec(wuk.shape, lambda i: (0, 0), pipeline_mode=pl.Buffered(1)),
            pl.BlockSpec(wuvt.shape, lambda i: (0, 0), pipeline_mode=pl.Buffered(1)),
            pl.BlockSpec((tm, LANES), lambda i: (i % tps, 0)),
            pl.BlockSpec((tm, LANES), lambda i: (i % tps, 0)),
        ],
        out_specs=[
            pl.BlockSpec((1, N_HEADS, tm, QK_PAD), lambda i: (i // tps, 0, i % tps, 0)),
            pl.BlockSpec((1, N_HEADS, 1, V_EXT, tm), lambda i: (i // tps, 0, i % tps, 0, 0)),
        ],
        out_shape=[
            jax.ShapeDtypeStruct((bsz, N_HEADS, seq, QK_PAD), BF16),
            jax.ShapeDtypeStruct((bsz, N_HEADS, tps, V_EXT, tm), BF16),
        ],
        compiler_params=_params(1),
        name="kv_proj",
    )(h, wd_ext, g, wuk, wuvt, cos_t, sin_t)


def _q_kernel(h_ref, wd_ref, g_ref, wut_ref, cos_ref, sin_ref, q_ref, *, n_sub):
    ts = h_ref.shape[0] // n_sub
    wd = wd_ref[...]
    lat = [_dot(h_ref[s * ts:(s + 1) * ts, :].astype(BF16), wd) for s in range(n_sub)]
    qts = []
    for s in range(n_sub):
        qc = _rms_norm(lat[s], g_ref[...]).astype(BF16)
        qts.append(_dot_nt(wut_ref[...], qc))
    nope_w = N_HEADS * QK_NOPE
    pe_w = N_HEADS * QK_ROPE
    for s in range(n_sub):
        cols = slice(s * ts, (s + 1) * ts)
        qt = qts[s]
        cos = cos_ref[:, cols]
        sin = sin_ref[:, cols]
        for pair in range(N_HEADS // 2):
            lo = nope_w + pair * LANES
            pe = (qt[lo:lo + LANES, :] * cos
                  + qt[lo + pe_w:lo + pe_w + LANES, :] * sin).astype(BF16)
            for hd in (2 * pair, 2 * pair + 1):
                q_ref[0, hd, :QK_NOPE, cols] = qt[hd * QK_NOPE:(hd + 1) * QK_NOPE, :].astype(BF16)
                q_ref[0, hd, QK_NOPE:, cols] = pe


def _q_proj(h, wd, g, wut_ext, j, cos_tt, sin_tt, *, bsz, seq, tm=512, n_sub=2):
    m, d = h.shape
    tps = seq // tm
    return pl.pallas_call(
        functools.partial(_q_kernel, n_sub=n_sub),
        grid=(m // tm,),
        in_specs=[
            pl.BlockSpec((tm, d), lambda i: (i, 0)),
            pl.BlockSpec((None,) + wd.shape[1:], lambda i: (j, 0, 0),
                         pipeline_mode=pl.Buffered(1)),
            pl.BlockSpec((None, 1, Q_LORA), lambda i: (j, 0, 0)),
            pl.BlockSpec((None,) + wut_ext.shape[1:], lambda i: (j, 0, 0),
                         pipeline_mode=pl.Buffered(1)),
            pl.BlockSpec((LANES, tm), lambda i: (0, i % tps)),
            pl.BlockSpec((LANES, tm), lambda i: (0, i % tps)),
        ],
        out_specs=pl.BlockSpec((1, N_HEADS, QK_PAD, tm), lambda i: (i // tps, 0, 0, i % tps)),
        out_shape=jax.ShapeDtypeStruct((bsz, N_HEADS, QK_PAD, seq), BF16),
        compiler_params=_params(1),
        name="q_proj",
    )(h, wd, g, wut_ext, cos_tt, sin_tt)


def _attn_blocks(seq, tb, cw):
    blocks = []
    for u in range(seq // tb):
        for ch in range(seq // cw):
            shift = ch * cw - u * tb
            if shift < 0:
                continue
            if shift == 0:
                blocks.append((u * tb, cw, ch, 0))
            else:
                blocks.append((u * tb, tb, ch, None if shift >= tb - 1 else shift))
    return blocks


def _attn_kernel(q_ref, k_ref, v_ref, *refs, tk, c):
    cast_src, (o_ref, *cast_dst) = refs[:N_CAST], refs[N_CAST:]
    cw = MXU_DIM
    seq = q_ref.shape[-1]
    blocks = _attn_blocks(seq, ATTN_ROWS, cw)
    _cast_step(cast_src, cast_dst)

    def scores(hd, blk):
        row0, rows, ch, _ = blk
        return _dot(k_ref[0, hd, row0:row0 + rows, :], q_ref[0, hd, :, ch * cw:(ch + 1) * cw])

    def chain_step(hd, state, s, blk):
        row0, rows, _, mask_shift = blk
        if mask_shift is not None:
            row = lax.broadcasted_iota(jnp.int32, s.shape, 0)
            col = lax.broadcasted_iota(jnp.int32, s.shape, 1)
            s = jnp.where(row <= col + mask_shift, s, -jnp.inf)
        vt = v_ref[0, hd, row0 // tk, :, row0 % tk:row0 % tk + rows]
        m_blk = jnp.max(s, axis=0, keepdims=True)
        if state is None:
            m_new = m_blk
            p = jnp.exp2(s * c - m_new * c)
            return m_new, _dot(vt, p.astype(BF16))
        m_prev, acc_prev = state
        m_new = jnp.maximum(m_prev, m_blk)
        p = jnp.exp2(s * c - m_new * c)
        corr = jnp.exp2((m_prev - m_new) * c)
        return m_new, corr * acc_prev + _dot(vt, p.astype(BF16))

    n_heads = q_ref.shape[1]
    work = [(hd, blk) for blk in blocks for hd in range(n_heads)]
    lookahead = ATTN_LOOKAHEAD * n_heads
    states = {(hd, ch): None for hd in range(n_heads) for ch in range(seq // cw)}
    pending = []
    for idx in range(len(work) + lookahead):
        if idx < len(work):
            pending.append(scores(*work[idx]))
        if idx >= lookahead:
            hd, blk = work[idx - lookahead]
            states[hd, blk[2]] = chain_step(hd, states[hd, blk[2]], pending.pop(0), blk)

    for (hd, ch), (_, acc) in states.items():
        out = (acc[:V_HEAD, :] * (1.0 / acc[V_HEAD:V_HEAD + 1, :])).T
        o_ref[0, ch * cw:(ch + 1) * cw, hd * V_HEAD:(hd + 1) * V_HEAD] = out.astype(BF16)


def _attention(qt, k, vt, cast_weights):
    bsz, nh, _, seq = qt.shape
    tk = ATTN_TK
    c = (QK_NOPE + QK_ROPE) ** -0.5 * math.log2(math.e)
    kern = functools.partial(_attn_kernel, tk=tk, c=c)
    hg = ATTN_HEADS
    ng = nh // hg
    c_in, c_out, c_shape = _cast_specs(cast_weights, bsz * ng, lambda b, h: b * ng + h)
    return pl.pallas_call(
        kern,
        grid=(bsz, ng),
        in_specs=[
            pl.BlockSpec((1, hg, QK_PAD, seq), lambda b, h: (b, h, 0, 0)),
            pl.BlockSpec((1, hg, seq, QK_PAD), lambda b, h: (b, h, 0, 0)),
            pl.BlockSpec((1, hg, seq // tk, V_EXT, tk), lambda b, h: (b, h, 0, 0, 0)),
        ] + c_in,
        out_specs=[pl.BlockSpec((1, seq, hg * V_HEAD), lambda b, h: (b, 0, h))] + c_out,
        out_shape=[jax.ShapeDtypeStruct((bsz, seq, nh * V_HEAD), BF16)] + c_shape,
        compiler_params=_params(2),
        name="attention",
    )(qt, k, vt, *[w for w, _ in cast_weights])


def _rotate_half_cols(w):
    half = w.shape[-1] // 2
    return jnp.concatenate([-w[..., half:], w[..., :half]], axis=-1)


def _rope_tables(seq):
    inv = 1.0 / (ROPE_THETA ** (jnp.arange(0, QK_ROPE, 2, dtype=F32) / QK_ROPE))
    ang = jnp.arange(seq, dtype=F32)[:, None] * inv[None, :]
    cos, sin = jnp.cos(ang), jnp.sin(ang)
    return jnp.tile(cos, (1, 4)), jnp.tile(sin, (1, 4))


def kernel(x, ln_g, ln_b, conv_w_in, conv_w, conv_w_out, kv_w_dkv, kv_norm_g, kv_w_ukv,
           mla_w_dq, mla_q_norm_g, mla_w_uq, mla_w_o, mlp_w1, mlp_w2):
    bsz, seq, d = x.shape
    m = bsz * seq
    h = x.reshape(m, d)
    cos_t, sin_t = _rope_tables(seq)
    cos_tt, sin_tt = cos_t.T, sin_t.T
    ln_g = ln_g.reshape(2 * DEPTH, 1, d)
    ln_b = ln_b.reshape(2 * DEPTH, 1, d)

    uq = mla_w_uq.astype(BF16).reshape(N_B, Q_LORA, N_HEADS, QK_NOPE + QK_ROPE)
    uq_pe = uq[..., QK_NOPE:]
    wut_ext = jnp.concatenate(
        [uq[..., :QK_NOPE].reshape(N_B, Q_LORA, -1), uq_pe.reshape(N_B, Q_LORA, -1),
         _rotate_half_cols(uq_pe).reshape(N_B, Q_LORA, -1)], axis=2).transpose(0, 2, 1)
    wdq = mla_w_dq.astype(BF16)
    q_norm_g = mla_q_norm_g.reshape(N_B, 1, Q_LORA)

    k_all = vt_all = None
    for layer in range(DEPTH):
        if layer < N_A:
            cast = [(conv_w_out, layer), (mlp_w1, layer), (mlp_w2, layer)]
            z, wo_bf, w1_bf, w2_bf = _conv_in(h, conv_w_in, conv_w, cast, layer, seq=seq)
            h = _proj_ln(z, wo_bf, h, ln_g, ln_b, 2 * layer)
        else:
            if layer == N_A:
                dkv = kv_w_dkv.astype(BF16)
                pe_w = dkv[:, KV_LORA:]
                rot_w = _rotate_half_cols(pe_w)
                wd_ext = jnp.concatenate([dkv[:, :KV_LORA], pe_w, pe_w, rot_w, rot_w], axis=1)
                wu = kv_w_ukv.astype(BF16).reshape(KV_LORA, N_HEADS, QK_NOPE + V_HEAD)
                wuk = wu[:, :, :QK_NOPE].reshape(KV_LORA, -1)
                wuvt = wu[:, :, QK_NOPE:].reshape(KV_LORA, -1).T
                k_all, vt_all = _kv_proj(h, wd_ext, kv_norm_g[None, :], wuk, wuvt,
                                         cos_t, sin_t, bsz=bsz, seq=seq)
            j = layer - N_A
            qt_all = _q_proj(h, wdq, q_norm_g, wut_ext, j, cos_tt, sin_tt, bsz=bsz, seq=seq)
            cast = [(mla_w_o, j), (mlp_w1, layer), (mlp_w2, layer)]
            o, wo_bf, w1_bf, w2_bf = _attention(qt_all, k_all, vt_all, cast)
            h = _proj_ln(o.reshape(m, N_HEADS * V_HEAD), wo_bf, h, ln_g, ln_b, 2 * layer)
        h = _mlp_ln(h, w1_bf, w2_bf, layer, ln_g, ln_b)
    return h.reshape(bsz, seq, d)
```

```python
import functools
import math

import jax
import jax.numpy as jnp
from jax import lax
from jax.experimental import pallas as pl
from jax.experimental.pallas import tpu as pltpu

D_MODEL = 2048
DEPTH = 4
N_A = DEPTH // 2
N_B = DEPTH - N_A
D_FF = 4 * D_MODEL
CONV_WIDTH = 3
N_HEADS = 16
QK_NOPE = 128
QK_ROPE = 64
V_HEAD = 128
Q_LORA = 512
KV_LORA = 512
ROPE_THETA = 10000.0
ALPHA = (2 * DEPTH) ** 0.25
LN_EPS = 1e-5
RMS_EPS = 1e-6

LANES = 128
SUBLANES = 8
MXU_DIM = 256
QK_PAD = 2 * LANES
VMEM_LIMIT = 56 * 1024 * 1024

ATTN_TK = 512
ATTN_ROWS = 512
ATTN_LOOKAHEAD = 3
ATTN_HEADS = 2
V_EXT = V_HEAD + 16

F32 = jnp.float32
BF16 = jnp.bfloat16
TN_NT_DIMS = (((0,), (1,)), ((), ()))


def _params(n_axes):
    return pltpu.CompilerParams(
        dimension_semantics=("arbitrary",) * n_axes, vmem_limit_bytes=VMEM_LIMIT)


def _dot(a, b):
    return jnp.dot(a, b, preferred_element_type=F32)


def _dot_tn_nt(a, b):
    return lax.dot_general(a, b, TN_NT_DIMS, preferred_element_type=F32)


def _layer_norm(y, g, b):
    mu = jnp.mean(y, axis=-1, keepdims=True)
    yc = y - mu
    var = jnp.mean(yc * yc, axis=-1, keepdims=True)
    return yc * lax.rsqrt(var + LN_EPS) * g + b


def _rms_norm(y, g):
    return y * lax.rsqrt(jnp.mean(y * y, axis=-1, keepdims=True) + RMS_EPS) * g


def _rotate_half_cols(w):
    half = w.shape[-1] // 2
    return jnp.concatenate([-w[:, half:], w[:, :half]], axis=-1)


N_CAST = 3


def _cast_specs(weights, n_steps, step_of):
    in_specs, out_specs, out_shape = [], [], []
    for w, idx in weights:
        rows, cols = w.shape[1] // n_steps, w.shape[2]
        in_specs.append(pl.BlockSpec((None, rows, cols),
                                     lambda *g, idx=idx: (idx, step_of(*g), 0)))
        out_specs.append(pl.BlockSpec((rows, cols), lambda *g: (step_of(*g), 0)))
        out_shape.append(jax.ShapeDtypeStruct(w.shape[1:], BF16))
    return in_specs, out_specs, out_shape


def _cast_step(src_refs, dst_refs):
    for src, dst in zip(src_refs, dst_refs):
        dst[...] = src[...].astype(BF16)


def _conv_in_kernel(h_ref, wb_ref, wc_ref, wu_ref, cw_ref, *refs, tm, tn, tiles_per_seq, n_sub):
    cast_src, (z_ref, *cast_dst), (w_bf, vbuf) = (
        refs[:N_CAST], refs[N_CAST:2 * N_CAST + 1], refs[2 * N_CAST + 1:])
    i = pl.program_id(1)
    _cast_step(cast_src, cast_dst)

    @pl.when(i == 0)
    def _():
        w_bf[:, :tn] = wb_ref[...].astype(BF16)
        w_bf[:, tn:2 * tn] = wc_ref[...].astype(BF16)
        w_bf[:, 2 * tn:] = wu_ref[...].astype(BF16)

    @pl.when(i % tiles_per_seq == 0)
    def _():
        vbuf[0:SUBLANES, :] = jnp.zeros((SUBLANES, tn), F32)

    @pl.when(i % tiles_per_seq != 0)
    def _():
        vbuf[0:SUBLANES, :] = vbuf[tm:tm + SUBLANES, :]

    ts = tm // n_sub
    w = w_bf[...]
    bcu = [_dot(h_ref[s * ts:(s + 1) * ts, :].astype(BF16), w) for s in range(n_sub)]
    cw = cw_ref[...]
    for s in range(n_sub):
        r0 = s * ts + SUBLANES
        gate_b = bcu[s][:, :tn]
        v = bcu[s][:, tn:2 * tn] * bcu[s][:, 2 * tn:]
        vbuf[r0:r0 + ts, :] = v
        y = (cw[2:3, :] * v
             + cw[1:2, :] * vbuf[r0 - 1:r0 + ts - 1, :]
             + cw[0:1, :] * vbuf[r0 - 2:r0 + ts - 2, :])
        z_ref[s * ts:(s + 1) * ts, :] = (gate_b * y).astype(BF16)


def _conv_in(h, w_in, conv_w, cast_weights, layer, *, seq, tm=512, tn=512, n_sub=2):
    m, d = h.shape
    nj, ni = d // tn, m // tm
    kern = functools.partial(_conv_in_kernel, tm=tm, tn=tn, tiles_per_seq=seq // tm, n_sub=n_sub)
    w_spec = [pl.BlockSpec((None, d, tn), lambda j, i, g=g: (layer, 0, g * nj + j))
              for g in range(3)]
    c_in, c_out, c_shape = _cast_specs(cast_weights, nj * ni, lambda j, i: j * ni + i)
    return pl.pallas_call(
        kern,
        grid=(nj, ni),
        in_specs=[pl.BlockSpec((tm, d), lambda j, i: (i, 0))] + w_spec + [
            pl.BlockSpec((None, CONV_WIDTH, tn), lambda j, i: (layer, 0, j))] + c_in,
        out_specs=[pl.BlockSpec((tm, tn), lambda j, i: (i, j))] + c_out,
        out_shape=[jax.ShapeDtypeStruct((m, d), BF16)] + c_shape,
        scratch_shapes=[pltpu.VMEM((d, 3 * tn), BF16),
                        pltpu.VMEM((tm + SUBLANES, tn), F32)],
        compiler_params=_params(2),
        name="conv_in",
    )(h, w_in, w_in, w_in, conv_w, *[w for w, _ in cast_weights])


def _proj_ln_kernel(z_ref, w_ref, h_ref, g_ref, b_ref, o_ref, *, n_sub):
    ts = o_ref.shape[0] // n_sub
    w = w_ref[...]
    acc = [_dot(z_ref[s * ts:(s + 1) * ts, :], w) for s in range(n_sub)]
    for s in range(n_sub):
        rows = slice(s * ts, (s + 1) * ts)
        y = ALPHA * h_ref[rows, :] + acc[s]
        o_ref[rows, :] = _layer_norm(y, g_ref[...], b_ref[...])


def _proj_ln(z, w, h, ln_g, ln_b, ln_row, *, tm=512, n_sub=2):
    m, k = z.shape
    d = w.shape[-1]
    return pl.pallas_call(
        functools.partial(_proj_ln_kernel, n_sub=n_sub),
        grid=(m // tm,),
        in_specs=[
            pl.BlockSpec((tm, k), lambda i: (i, 0)),
            pl.BlockSpec((k, d), lambda i: (0, 0), pipeline_mode=pl.Buffered(1)),
            pl.BlockSpec((tm, d), lambda i: (i, 0)),
            pl.BlockSpec((None, 1, d), lambda i: (ln_row, 0, 0)),
            pl.BlockSpec((None, 1, d), lambda i: (ln_row, 0, 0)),
        ],
        out_specs=pl.BlockSpec((tm, d), lambda i: (i, 0)),
        out_shape=jax.ShapeDtypeStruct((m, d), F32),
        compiler_params=_params(1),
        name="proj_ln",
    )(z, w, h, ln_g, ln_b)


def _mlp_ln_kernel(h_hbm, w1_ref, w2_ref, g_ref, b_ref, o_ref, xb_ref, h_buf, h_sem,
                   *, ln_rows):
    i = pl.program_id(0)
    k = pl.program_id(1)
    tm = o_ref.shape[0]

    def h_copy(tile):
        rows = pl.ds(pl.multiple_of(tile * tm, tm), tm)
        return pltpu.make_async_copy(h_hbm.at[rows, :], h_buf, h_sem)

    @pl.when((i == 0) & (k == 0))
    def _():
        h_copy(0).start()

    @pl.when(k == 0)
    def _():
        h_copy(i).wait()
        xb_ref[...] = h_buf[...].astype(BF16)
        o_ref[...] = ALPHA * h_buf[...]

    @pl.when((k == 1) & (i + 1 < pl.num_programs(0)))
    def _():
        h_copy(i + 1).start()

    a = jnp.maximum(_dot(xb_ref[...], w1_ref[...]), 0.0)
    o_ref[...] += _dot((a * a).astype(BF16), w2_ref[...])

    @pl.when(k == pl.num_programs(1) - 1)
    def _():
        def ln_chunk(r, carry):
            rows = pl.ds(pl.multiple_of(r * ln_rows, ln_rows), ln_rows)
            o_ref[rows, :] = _layer_norm(o_ref[rows, :], g_ref[...], b_ref[...])
            return carry

        lax.fori_loop(0, tm // ln_rows, ln_chunk, 0)


def _mlp_ln(h, w1, w2, layer, ln_g, ln_b, *, tm=1024, tf=1024, ln_rows=256):
    m, d = h.shape
    f = w1.shape[-1]
    ln_row = 2 * layer + 1
    return pl.pallas_call(
        functools.partial(_mlp_ln_kernel, ln_rows=ln_rows),
        grid=(m // tm, f // tf),
        in_specs=[
            pl.BlockSpec(memory_space=pl.ANY),
            pl.BlockSpec((d, tf), lambda i, k: (0, k)),
            pl.BlockSpec((tf, d), lambda i, k: (k, 0)),
            pl.BlockSpec((None, 1, d), lambda i, k: (ln_row, 0, 0)),
            pl.BlockSpec((None, 1, d), lambda i, k: (ln_row, 0, 0)),
        ],
        out_specs=pl.BlockSpec((tm, d), lambda i, k: (i, 0)),
        out_shape=jax.ShapeDtypeStruct((m, d), F32),
        scratch_shapes=[pltpu.VMEM((tm, d), BF16), pltpu.VMEM((tm, d), F32),
                        pltpu.SemaphoreType.DMA(())],
        compiler_params=_params(2),
        name="mlp_ln",
    )(h, w1, w2, ln_g, ln_b)


def _kv_kernel(h_ref, wd_ref, g_ref, wu_ref, cos_ref, sin_ref, k_ref, v_ref,
               wd_bf, wuk_bf, wuv_bf, *, n_sub):
    @pl.when(pl.program_id(0) == 0)
    def _():
        wd_bf[:, :KV_LORA] = wd_ref[:, :KV_LORA].astype(BF16)
        pe = wd_ref[:, KV_LORA:]
        rot = _rotate_half_cols(pe)
        for rep in range(2):
            lo = KV_LORA + rep * QK_ROPE
            wd_bf[:, lo:lo + QK_ROPE] = pe.astype(BF16)
            wd_bf[:, lo + LANES:lo + LANES + QK_ROPE] = rot.astype(BF16)
        for hd in range(N_HEADS):
            base = hd * (QK_NOPE + V_HEAD)
            wuk_bf[:, hd * QK_NOPE:(hd + 1) * QK_NOPE] = (
                wu_ref[:, base:base + QK_NOPE].astype(BF16))
            wuv_bf[:, hd * V_HEAD:(hd + 1) * V_HEAD] = (
                wu_ref[:, base + QK_NOPE:base + QK_NOPE + V_HEAD].astype(BF16))

    ts = h_ref.shape[0] // n_sub
    wd = wd_bf[...]
    ckv = [_dot(h_ref[s * ts:(s + 1) * ts, :].astype(BF16), wd)
           for s in range(n_sub)]
    kn, vt = [], []
    for s in range(n_sub):
        c = _rms_norm(ckv[s][:, :KV_LORA], g_ref[...]).astype(BF16)
        kn.append(_dot(c, wuk_bf[...]))
        vt.append(_dot_tn_nt(wuv_bf[...], c))
    for s in range(n_sub):
        rows = slice(s * ts, (s + 1) * ts)
        pe = (ckv[s][:, KV_LORA:KV_LORA + LANES] * cos_ref[rows, :]
              + ckv[s][:, KV_LORA + LANES:] * sin_ref[rows, :])
        lane = lax.broadcasted_iota(jnp.int32, pe.shape, 1)
        pe_lo = jnp.where(lane < QK_ROPE, pe, 0.0).astype(BF16)
        pe_hi = jnp.where(lane >= QK_ROPE, pe, 0.0).astype(BF16)
        for hd in range(N_HEADS):
            k_ref[0, hd, rows, :QK_NOPE] = kn[s][:, hd * QK_NOPE:(hd + 1) * QK_NOPE].astype(BF16)
            k_ref[0, hd, rows, QK_NOPE:] = pe_lo if hd % 2 == 0 else pe_hi
            v_ref[0, hd, 0, :V_HEAD, rows] = vt[s][hd * V_HEAD:(hd + 1) * V_HEAD, :].astype(BF16)
            v_ref[0, hd, 0, V_HEAD:, rows] = jnp.ones((V_EXT - V_HEAD, ts), BF16)


def _kv_proj(h, wd, g, wu, cos_t, sin_t, *, bsz, seq, n_sub=2):
    m, d = h.shape
    tm = ATTN_TK
    tps = seq // tm
    return pl.pallas_call(
        functools.partial(_kv_kernel, n_sub=n_sub),
        grid=(m // tm,),
        in_specs=[
            pl.BlockSpec((tm, d), lambda i: (i, 0)),
            pl.BlockSpec(wd.shape, lambda i: (0, 0), pipeline_mode=pl.Buffered(1)),
            pl.BlockSpec((1, KV_LORA), lambda i: (0, 0)),
            pl.BlockSpec(wu.shape, lambda i: (0, 0), pipeline_mode=pl.Buffered(1)),
            pl.BlockSpec((tm, LANES), lambda i: (i % tps, 0)),
            pl.BlockSpec((tm, LANES), lambda i: (i % tps, 0)),
        ],
        out_specs=[
            pl.BlockSpec((1, N_HEADS, tm, QK_PAD), lambda i: (i // tps, 0, i % tps, 0)),
            pl.BlockSpec((1, N_HEADS, 1, V_EXT, tm), lambda i: (i // tps, 0, i % tps, 0, 0)),
        ],
        out_shape=[
            jax.ShapeDtypeStruct((bsz, N_HEADS, seq, QK_PAD), BF16),
            jax.ShapeDtypeStruct((bsz, N_HEADS, tps, V_EXT, tm), BF16),
        ],
        scratch_shapes=[pltpu.VMEM((d, KV_LORA + 2 * LANES), BF16),
                        pltpu.VMEM((KV_LORA, N_HEADS * QK_NOPE), BF16),
                        pltpu.VMEM((KV_LORA, N_HEADS * V_HEAD), BF16)],
        compiler_params=_params(1),
        name="kv_proj",
    )(h, wd, g, wu, cos_t, sin_t)


def _q_kernel(h_ref, wd_ref, g_ref, wu_ref, cos_ref, sin_ref, q_ref, wd_bf, wu_bf, *, n_sub):
    nope_w = N_HEADS * QK_NOPE
    pe_w = N_HEADS * QK_ROPE
    head_w = QK_NOPE + QK_ROPE

    @pl.when(pl.program_id(0) == 0)
    def _():
        wd_bf[...] = wd_ref[...].astype(BF16)
        for hd in range(N_HEADS):
            w = wu_ref[:, hd * head_w:(hd + 1) * head_w]
            pe = w[:, QK_NOPE:]
            rot = _rotate_half_cols(pe)
            wu_bf[:, hd * QK_NOPE:(hd + 1) * QK_NOPE] = w[:, :QK_NOPE].astype(BF16)
            wu_bf[:, nope_w + hd * QK_ROPE:nope_w + (hd + 1) * QK_ROPE] = pe.astype(BF16)
            wu_bf[:, nope_w + pe_w + hd * QK_ROPE:nope_w + pe_w + (hd + 1) * QK_ROPE] = (
                rot.astype(BF16))

    ts = h_ref.shape[0] // n_sub
    wd = wd_bf[...]
    lat = [_dot(h_ref[s * ts:(s + 1) * ts, :].astype(BF16), wd) for s in range(n_sub)]
    qts = []
    for s in range(n_sub):
        qc = _rms_norm(lat[s], g_ref[...]).astype(BF16)
        qts.append(_dot_tn_nt(wu_bf[...], qc))
    for s in range(n_sub):
        cols = slice(s * ts, (s + 1) * ts)
        qt = qts[s]
        cos = cos_ref[:, cols]
        sin = sin_ref[:, cols]
        for pair in range(N_HEADS // 2):
            lo = nope_w + pair * LANES
            pe = (qt[lo:lo + LANES, :] * cos
                  + qt[lo + pe_w:lo + pe_w + LANES, :] * sin).astype(BF16)
            for hd in (2 * pair, 2 * pair + 1):
                q_ref[0, hd, :QK_NOPE, cols] = qt[hd * QK_NOPE:(hd + 1) * QK_NOPE, :].astype(BF16)
                q_ref[0, hd, QK_NOPE:, cols] = pe


def _q_proj(h, wd, g, wu, j, cos_tt, sin_tt, *, bsz, seq, tm=512, n_sub=2):
    m, d = h.shape
    tps = seq // tm
    ext_w = N_HEADS * (QK_NOPE + 2 * QK_ROPE)
    return pl.pallas_call(
        functools.partial(_q_kernel, n_sub=n_sub),
        grid=(m // tm,),
        in_specs=[
            pl.BlockSpec((tm, d), lambda i: (i, 0)),
            pl.BlockSpec((None,) + wd.shape[1:], lambda i: (j, 0, 0),
                         pipeline_mode=pl.Buffered(1)),
            pl.BlockSpec((None, 1, Q_LORA), lambda i: (j, 0, 0)),
            pl.BlockSpec((None,) + wu.shape[1:], lambda i: (j, 0, 0),
                         pipeline_mode=pl.Buffered(1)),
            pl.BlockSpec((LANES, tm), lambda i: (0, i % tps)),
            pl.BlockSpec((LANES, tm), lambda i: (0, i % tps)),
        ],
        out_specs=pl.BlockSpec((1, N_HEADS, QK_PAD, tm), lambda i: (i // tps, 0, 0, i % tps)),
        out_shape=jax.ShapeDtypeStruct((bsz, N_HEADS, QK_PAD, seq), BF16),
        scratch_shapes=[pltpu.VMEM(wd.shape[1:], BF16), pltpu.VMEM((Q_LORA, ext_w), BF16)],
        compiler_params=_params(1),
        name="q_proj",
    )(h, wd, g, wu, cos_tt, sin_tt)


def _attn_blocks(seq, tb, cw):
    blocks = []
    for u in range(seq // tb):
        for ch in range(seq // cw):
            shift = ch * cw - u * tb
            if shift < 0:
                continue
            if shift == 0:
                blocks.append((u * tb, cw, ch, 0))
            else:
                blocks.append((u * tb, tb, ch, None if shift >= tb - 1 else shift))
    return blocks


def _attn_kernel(q_ref, k_ref, v_ref, *refs, tk, c):
    cast_src, (o_ref, *cast_dst) = refs[:N_CAST], refs[N_CAST:]
    cw = MXU_DIM
    seq = q_ref.shape[-1]
    blocks = _attn_blocks(seq, ATTN_ROWS, cw)
    _cast_step(cast_src, cast_dst)

    def scores(hd, blk):
        row0, rows, ch, _ = blk
        return _dot(k_ref[0, hd, row0:row0 + rows, :], q_ref[0, hd, :, ch * cw:(ch + 1) * cw])

    def chain_step(hd, state, s, blk):
        row0, rows, _, mask_shift = blk
        if mask_shift is not None:
            row = lax.broadcasted_iota(jnp.int32, s.shape, 0)
            col = lax.broadcasted_iota(jnp.int32, s.shape, 1)
            s = jnp.where(row <= col + mask_shift, s, -jnp.inf)
        vt = v_ref[0, hd, row0 // tk, :, row0 % tk:row0 % tk + rows]
        m_blk = jnp.max(s, axis=0, keepdims=True)
        if state is None:
            m_new = m_blk
            p = jnp.exp2(s * c - m_new * c)
            return m_new, _dot(vt, p.astype(BF16))
        m_prev, acc_prev = state
        m_new = jnp.maximum(m_prev, m_blk)
        p = jnp.exp2(s * c - m_new * c)
        corr = jnp.exp2((m_prev - m_new) * c)
        return m_new, corr * acc_prev + _dot(vt, p.astype(BF16))

    n_heads = q_ref.shape[1]
    work = [(hd, blk) for blk in blocks for hd in range(n_heads)]
    lookahead = ATTN_LOOKAHEAD * n_heads
    states = {(hd, ch): None for hd in range(n_heads) for ch in range(seq // cw)}
    pending = []
    for idx in range(len(work) + lookahead):
        if idx < len(work):
            pending.append(scores(*work[idx]))
        if idx >= lookahead:
            hd, blk = work[idx - lookahead]
            states[hd, blk[2]] = chain_step(hd, states[hd, blk[2]], pending.pop(0), blk)

    for (hd, ch), (_, acc) in states.items():
        out = (acc[:V_HEAD, :] * (1.0 / acc[V_HEAD:V_HEAD + 1, :])).T
        o_ref[0, ch * cw:(ch + 1) * cw, hd * V_HEAD:(hd + 1) * V_HEAD] = out.astype(BF16)


def _attention(qt, k, vt, cast_weights):
    bsz, nh, _, seq = qt.shape
    tk = ATTN_TK
    c = (QK_NOPE + QK_ROPE) ** -0.5 * math.log2(math.e)
    kern = functools.partial(_attn_kernel, tk=tk, c=c)
    hg = ATTN_HEADS
    ng = nh // hg
    c_in, c_out, c_shape = _cast_specs(cast_weights, bsz * ng, lambda b, h: b * ng + h)
    return pl.pallas_call(
        kern,
        grid=(bsz, ng),
        in_specs=[
            pl.BlockSpec((1, hg, QK_PAD, seq), lambda b, h: (b, h, 0, 0)),
            pl.BlockSpec((1, hg, seq, QK_PAD), lambda b, h: (b, h, 0, 0)),
            pl.BlockSpec((1, hg, seq // tk, V_EXT, tk), lambda b, h: (b, h, 0, 0, 0)),
        ] + c_in,
        out_specs=[pl.BlockSpec((1, seq, hg * V_HEAD), lambda b, h: (b, 0, h))] + c_out,
        out_shape=[jax.ShapeDtypeStruct((bsz, seq, nh * V_HEAD), BF16)] + c_shape,
        compiler_params=_params(2),
        name="attention",
    )(qt, k, vt, *[w for w, _ in cast_weights])


def _rope_tables(seq):
    inv = 1.0 / (ROPE_THETA ** (jnp.arange(0, QK_ROPE, 2, dtype=F32) / QK_ROPE))
    ang = jnp.arange(seq, dtype=F32)[:, None] * inv[None, :]
    cos, sin = jnp.cos(ang), jnp.sin(ang)
    return jnp.tile(cos, (1, 4)), jnp.tile(sin, (1, 4))


def kernel(x, ln_g, ln_b, conv_w_in, conv_w, conv_w_out, kv_w_dkv, kv_norm_g, kv_w_ukv,
           mla_w_dq, mla_q_norm_g, mla_w_uq, mla_w_o, mlp_w1, mlp_w2):
    bsz, seq, d = x.shape
    m = bsz * seq
    h = x.reshape(m, d)
    cos_t, sin_t = _rope_tables(seq)
    cos_tt, sin_tt = cos_t.T, sin_t.T
    ln_g = ln_g.reshape(2 * DEPTH, 1, d)
    ln_b = ln_b.reshape(2 * DEPTH, 1, d)

    q_norm_g = mla_q_norm_g.reshape(N_B, 1, Q_LORA)

    k_all = vt_all = None
    for layer in range(DEPTH):
        if layer < N_A:
            cast = [(conv_w_out, layer), (mlp_w1, layer), (mlp_w2, layer)]
            z, wo_bf, w1_bf, w2_bf = _conv_in(h, conv_w_in, conv_w, cast, layer, seq=seq)
            h = _proj_ln(z, wo_bf, h, ln_g, ln_b, 2 * layer)
        else:
            if layer == N_A:
                k_all, vt_all = _kv_proj(h, kv_w_dkv, kv_norm_g[None, :], kv_w_ukv,
                                         cos_t, sin_t, bsz=bsz, seq=seq)
            j = layer - N_A
            qt_all = _q_proj(h, mla_w_dq, q_norm_g, mla_w_uq, j, cos_tt, sin_tt,
                             bsz=bsz, seq=seq)
            cast = [(mla_w_o, j), (mlp_w1, layer), (mlp_w2, layer)]
            o, wo_bf, w1_bf, w2_bf = _attention(qt_all, k_all, vt_all, cast)
            h = _proj_ln(o.reshape(m, N_HEADS * V_HEAD), wo_bf, h, ln_g, ln_b, 2 * layer)
        h = _mlp_ln(h, w1_bf, w2_bf, layer, ln_g, ln_b)
    return h.reshape(bsz, seq, d)
```

```python
import functools
import math

import jax
import jax.numpy as jnp
from jax import lax
from jax.experimental import pallas as pl
from jax.experimental.pallas import tpu as pltpu

D_MODEL = 2048
DEPTH = 4
N_A = DEPTH // 2
N_B = DEPTH - N_A
D_FF = 4 * D_MODEL
CONV_WIDTH = 3
N_HEADS = 16
QK_NOPE = 128
QK_ROPE = 64
V_HEAD = 128
Q_LORA = 512
KV_LORA = 512
ROPE_THETA = 10000.0
ALPHA = (2 * DEPTH) ** 0.25
LN_EPS = 1e-5
RMS_EPS = 1e-6

LANES = 128
SUBLANES = 8
MXU_DIM = 256
QK_PAD = 2 * LANES
VMEM_LIMIT = 56 * 1024 * 1024

ATTN_TK = 512
ATTN_ROWS = 512
ATTN_LOOKAHEAD = 3
ATTN_HEADS = 2
V_EXT = V_HEAD + 16

F32 = jnp.float32
BF16 = jnp.bfloat16
TN_NT_DIMS = (((0,), (1,)), ((), ()))


def _params(n_axes):
    return pltpu.CompilerParams(
        dimension_semantics=("arbitrary",) * n_axes, vmem_limit_bytes=VMEM_LIMIT)


def _dot(a, b):
    return jnp.dot(a, b, preferred_element_type=F32)


def _dot_tn_nt(a, b):
    return lax.dot_general(a, b, TN_NT_DIMS, preferred_element_type=F32)


def _layer_norm(y, g, b):
    mu = jnp.mean(y, axis=-1, keepdims=True)
    yc = y - mu
    var = jnp.mean(yc * yc, axis=-1, keepdims=True)
    return yc * lax.rsqrt(var + LN_EPS) * g + b


def _rms_norm(y, g):
    return y * lax.rsqrt(jnp.mean(y * y, axis=-1, keepdims=True) + RMS_EPS) * g


def _rotate_half_cols(w):
    half = w.shape[-1] // 2
    return jnp.concatenate([-w[:, half:], w[:, :half]], axis=-1)


N_CAST = 3


def _cast_specs(weights, n_steps, step_of):
    in_specs, out_specs, out_shape = [], [], []
    for w, idx in weights:
        rows, cols = w.shape[1] // n_steps, w.shape[2]
        in_specs.append(pl.BlockSpec((None, rows, cols),
                                     lambda *g, idx=idx: (idx, step_of(*g), 0)))
        out_specs.append(pl.BlockSpec((rows, cols), lambda *g: (step_of(*g), 0)))
        out_shape.append(jax.ShapeDtypeStruct(w.shape[1:], BF16))
    return in_specs, out_specs, out_shape


def _cast_step(src_refs, dst_refs):
    for src, dst in zip(src_refs, dst_refs):
        dst[...] = src[...].astype(BF16)


def _conv_in_kernel(h_ref, wb_ref, wc_ref, wu_ref, cw_ref, *refs, tm, tn, tiles_per_seq, n_sub):
    cast_src, (z_ref, *cast_dst), (w_bf, vbuf) = (
        refs[:N_CAST], refs[N_CAST:2 * N_CAST + 1], refs[2 * N_CAST + 1:])
    i = pl.program_id(1)
    _cast_step(cast_src, cast_dst)

    @pl.when(i == 0)
    def _():
        w_bf[:, :tn] = wb_ref[...].astype(BF16)
        w_bf[:, tn:2 * tn] = wc_ref[...].astype(BF16)
        w_bf[:, 2 * tn:] = wu_ref[...].astype(BF16)

    @pl.when(i % tiles_per_seq == 0)
    def _():
        vbuf[0:SUBLANES, :] = jnp.zeros((SUBLANES, tn), F32)

    @pl.when(i % tiles_per_seq != 0)
    def _():
        vbuf[0:SUBLANES, :] = vbuf[tm:tm + SUBLANES, :]

    ts = tm // n_sub
    w = w_bf[...]
    bcu = [_dot(h_ref[s * ts:(s + 1) * ts, :].astype(BF16), w) for s in range(n_sub)]
    cw = cw_ref[...]
    for s in range(n_sub):
        r0 = s * ts + SUBLANES
        gate_b = bcu[s][:, :tn]
        v = bcu[s][:, tn:2 * tn] * bcu[s][:, 2 * tn:]
        vbuf[r0:r0 + ts, :] = v
        y = (cw[2:3, :] * v
             + cw[1:2, :] * vbuf[r0 - 1:r0 + ts - 1, :]
             + cw[0:1, :] * vbuf[r0 - 2:r0 + ts - 2, :])
        z_ref[s * ts:(s + 1) * ts, :] = (gate_b * y).astype(BF16)


def _conv_in(h, w_in, conv_w, cast_weights, layer, *, seq, tm=512, tn=512, n_sub=1):
    m, d = h.shape
    nj, ni = d // tn, m // tm
    kern = functools.partial(_conv_in_kernel, tm=tm, tn=tn, tiles_per_seq=seq // tm, n_sub=n_sub)
    w_spec = [pl.BlockSpec((None, d, tn), lambda j, i, g=g: (layer, 0, g * nj + j))
              for g in range(3)]
    c_in, c_out, c_shape = _cast_specs(cast_weights, nj * ni, lambda j, i: j * ni + i)
    return pl.pallas_call(
        kern,
        grid=(nj, ni),
        in_specs=[pl.BlockSpec((tm, d), lambda j, i: (i, 0))] + w_spec + [
            pl.BlockSpec((None, CONV_WIDTH, tn), lambda j, i: (layer, 0, j))] + c_in,
        out_specs=[pl.BlockSpec((tm, tn), lambda j, i: (i, j))] + c_out,
        out_shape=[jax.ShapeDtypeStruct((m, d), BF16)] + c_shape,
        scratch_shapes=[pltpu.VMEM((d, 3 * tn), BF16),
                        pltpu.VMEM((tm + SUBLANES, tn), F32)],
        compiler_params=_params(2),
        name="conv_in",
    )(h, w_in, w_in, w_in, conv_w, *[w for w, _ in cast_weights])


def _proj_ln_kernel(z_ref, w_ref, h_ref, g_ref, b_ref, o_ref, *, n_sub):
    ts = o_ref.shape[0] // n_sub
    w = w_ref[...]
    acc = [_dot(z_ref[s * ts:(s + 1) * ts, :], w) for s in range(n_sub)]
    for s in range(n_sub):
        rows = slice(s * ts, (s + 1) * ts)
        y = ALPHA * h_ref[rows, :] + acc[s]
        o_ref[rows, :] = _layer_norm(y, g_ref[0:1, :], b_ref[0:1, :])


def _proj_ln(z, w, h, ln_g, ln_b, layer, *, tm=512, n_sub=2):
    m, k = z.shape
    d = w.shape[-1]
    return pl.pallas_call(
        functools.partial(_proj_ln_kernel, n_sub=n_sub),
        grid=(m // tm,),
        in_specs=[
            pl.BlockSpec((tm, k), lambda i: (i, 0)),
            pl.BlockSpec((k, d), lambda i: (0, 0), pipeline_mode=pl.Buffered(1)),
            pl.BlockSpec((tm, d), lambda i: (i, 0)),
            pl.BlockSpec((None, 2, d), lambda i: (layer, 0, 0)),
            pl.BlockSpec((None, 2, d), lambda i: (layer, 0, 0)),
        ],
        out_specs=pl.BlockSpec((tm, d), lambda i: (i, 0)),
        out_shape=jax.ShapeDtypeStruct((m, d), F32),
        compiler_params=_params(1),
        name="proj_ln",
    )(z, w, h, ln_g, ln_b)


def _mlp_ln_kernel(h_hbm, w1_ref, w2_ref, g_ref, b_ref, o_ref, xb_ref, h_buf, h_sem,
                   *, ln_rows):
    i = pl.program_id(0)
    k = pl.program_id(1)
    tm = o_ref.shape[0]

    def h_copy(tile):
        rows = pl.ds(pl.multiple_of(tile * tm, tm), tm)
        return pltpu.make_async_copy(h_hbm.at[rows, :], h_buf, h_sem)

    @pl.when((i == 0) & (k == 0))
    def _():
        h_copy(0).start()

    @pl.when(k == 0)
    def _():
        h_copy(i).wait()
        xb_ref[...] = h_buf[...].astype(BF16)
        o_ref[...] = ALPHA * h_buf[...]

    @pl.when((k == 1) & (i + 1 < pl.num_programs(0)))
    def _():
        h_copy(i + 1).start()

    a = jnp.maximum(_dot(xb_ref[...], w1_ref[...]), 0.0)
    o_ref[...] += _dot((a * a).astype(BF16), w2_ref[...])

    @pl.when(k == pl.num_programs(1) - 1)
    def _():
        def ln_chunk(r, carry):
            rows = pl.ds(pl.multiple_of(r * ln_rows, ln_rows), ln_rows)
            o_ref[rows, :] = _layer_norm(o_ref[rows, :], g_ref[1:2, :], b_ref[1:2, :])
            return carry

        lax.fori_loop(0, tm // ln_rows, ln_chunk, 0)


def _mlp_ln(h, w1, w2, layer, ln_g, ln_b, *, tm=1024, tf=1024, ln_rows=256):
    m, d = h.shape
    f = w1.shape[-1]
    return pl.pallas_call(
        functools.partial(_mlp_ln_kernel, ln_rows=ln_rows),
        grid=(m // tm, f // tf),
        in_specs=[
            pl.BlockSpec(memory_space=pl.ANY),
            pl.BlockSpec((d, tf), lambda i, k: (0, k)),
            pl.BlockSpec((tf, d), lambda i, k: (k, 0)),
            pl.BlockSpec((None, 2, d), lambda i, k: (layer, 0, 0)),
            pl.BlockSpec((None, 2, d), lambda i, k: (layer, 0, 0)),
        ],
        out_specs=pl.BlockSpec((tm, d), lambda i, k: (i, 0)),
        out_shape=jax.ShapeDtypeStruct((m, d), F32),
        scratch_shapes=[pltpu.VMEM((tm, d), BF16), pltpu.VMEM((tm, d), F32),
                        pltpu.SemaphoreType.DMA(())],
        compiler_params=_params(2),
        name="mlp_ln",
    )(h, w1, w2, ln_g, ln_b)


def _kv_kernel(h_ref, wd_ref, g_ref, wu_ref, cos_ref, sin_ref, k_ref, v_ref,
               wd_bf, wuk_bf, wuv_bf, *, n_sub):
    @pl.when(pl.program_id(0) == 0)
    def _():
        wd_bf[:, :KV_LORA] = wd_ref[:, :KV_LORA].astype(BF16)
        pe = wd_ref[:, KV_LORA:]
        rot = _rotate_half_cols(pe)
        for rep in range(2):
            lo = KV_LORA + rep * QK_ROPE
            wd_bf[:, lo:lo + QK_ROPE] = pe.astype(BF16)
            wd_bf[:, lo + LANES:lo + LANES + QK_ROPE] = rot.astype(BF16)
        for hd in range(N_HEADS):
            base = hd * (QK_NOPE + V_HEAD)
            wuk_bf[:, hd * QK_NOPE:(hd + 1) * QK_NOPE] = (
                wu_ref[:, base:base + QK_NOPE].astype(BF16))
            wuv_bf[:, hd * V_HEAD:(hd + 1) * V_HEAD] = (
                wu_ref[:, base + QK_NOPE:base + QK_NOPE + V_HEAD].astype(BF16))

    ts = h_ref.shape[0] // n_sub
    wd = wd_bf[...]
    ckv = [_dot(h_ref[s * ts:(s + 1) * ts, :].astype(BF16), wd)
           for s in range(n_sub)]
    kn, vt = [], []
    for s in range(n_sub):
        c = _rms_norm(ckv[s][:, :KV_LORA], g_ref[...]).astype(BF16)
        kn.append(_dot(c, wuk_bf[...]))
        vt.append(_dot_tn_nt(wuv_bf[...], c))
    for s in range(n_sub):
        rows = slice(s * ts, (s + 1) * ts)
        pe = (ckv[s][:, KV_LORA:KV_LORA + LANES] * cos_ref[rows, :]
              + ckv[s][:, KV_LORA + LANES:] * sin_ref[rows, :])
        lane = lax.broadcasted_iota(jnp.int32, pe.shape, 1)
        pe_lo = jnp.where(lane < QK_ROPE, pe, 0.0).astype(BF16)
        pe_hi = jnp.where(lane >= QK_ROPE, pe, 0.0).astype(BF16)
        for hd in range(N_HEADS):
            k_ref[0, hd, rows, :QK_NOPE] = kn[s][:, hd * QK_NOPE:(hd + 1) * QK_NOPE].astype(BF16)
            k_ref[0, hd, rows, QK_NOPE:] = pe_lo if hd % 2 == 0 else pe_hi
            v_ref[0, hd, 0, :V_HEAD, rows] = vt[s][hd * V_HEAD:(hd + 1) * V_HEAD, :].astype(BF16)
            v_ref[0, hd, 0, V_HEAD:, rows] = jnp.ones((V_EXT - V_HEAD, ts), BF16)


def _kv_proj(h, wd, g, wu, cos_t, sin_t, *, bsz, seq, n_sub=1):
    m, d = h.shape
    tm = ATTN_TK
    tps = seq // tm
    return pl.pallas_call(
        functools.partial(_kv_kernel, n_sub=n_sub),
        grid=(m // tm,),
        in_specs=[
            pl.BlockSpec((tm, d), lambda i: (i, 0)),
            pl.BlockSpec(wd.shape, lambda i: (0, 0), pipeline_mode=pl.Buffered(1)),
            pl.BlockSpec((1, KV_LORA), lambda i: (0, 0)),
            pl.BlockSpec(wu.shape, lambda i: (0, 0), pipeline_mode=pl.Buffered(1)),
            pl.BlockSpec((tm, LANES), lambda i: (i % tps, 0)),
            pl.BlockSpec((tm, LANES), lambda i: (i % tps, 0)),
        ],
        out_specs=[
            pl.BlockSpec((1, N_HEADS, tm, QK_PAD), lambda i: (i // tps, 0, i % tps, 0)),
            pl.BlockSpec((1, N_HEADS, 1, V_EXT, tm), lambda i: (i // tps, 0, i % tps, 0, 0)),
        ],
        out_shape=[
            jax.ShapeDtypeStruct((bsz, N_HEADS, seq, QK_PAD), BF16),
            jax.ShapeDtypeStruct((bsz, N_HEADS, tps, V_EXT, tm), BF16),
        ],
        scratch_shapes=[pltpu.VMEM((d, KV_LORA + 2 * LANES), BF16),
                        pltpu.VMEM((KV_LORA, N_HEADS * QK_NOPE), BF16),
                        pltpu.VMEM((KV_LORA, N_HEADS * V_HEAD), BF16)],
        compiler_params=_params(1),
        name="kv_proj",
    )(h, wd, g, wu, cos_t, sin_t)


def _q_kernel(h_ref, wd_ref, g_ref, wu_ref, cos_ref, sin_ref, q_ref, wd_bf, wu_bf, *, n_sub, j):
    nope_w = N_HEADS * QK_NOPE
    pe_w = N_HEADS * QK_ROPE
    head_w = QK_NOPE + QK_ROPE

    @pl.when(pl.program_id(0) == 0)
    def _():
        wd_bf[...] = wd_ref[...].astype(BF16)
        for hd in range(N_HEADS):
            w = wu_ref[:, hd * head_w:(hd + 1) * head_w]
            pe = w[:, QK_NOPE:]
            rot = _rotate_half_cols(pe)
            wu_bf[:, hd * QK_NOPE:(hd + 1) * QK_NOPE] = w[:, :QK_NOPE].astype(BF16)
            wu_bf[:, nope_w + hd * QK_ROPE:nope_w + (hd + 1) * QK_ROPE] = pe.astype(BF16)
            wu_bf[:, nope_w + pe_w + hd * QK_ROPE:nope_w + pe_w + (hd + 1) * QK_ROPE] = (
                rot.astype(BF16))

    ts = h_ref.shape[0] // n_sub
    wd = wd_bf[...]
    lat = [_dot(h_ref[s * ts:(s + 1) * ts, :].astype(BF16), wd) for s in range(n_sub)]
    qts = []
    for s in range(n_sub):
        qc = _rms_norm(lat[s], g_ref[j:j + 1, :]).astype(BF16)
        qts.append(_dot_tn_nt(wu_bf[...], qc))
    for s in range(n_sub):
        cols = slice(s * ts, (s + 1) * ts)
        qt = qts[s]
        cos = cos_ref[:, cols]
        sin = sin_ref[:, cols]
        for pair in range(N_HEADS // 2):
            lo = nope_w + pair * LANES
            pe = (qt[lo:lo + LANES, :] * cos
                  + qt[lo + pe_w:lo + pe_w + LANES, :] * sin).astype(BF16)
            for hd in (2 * pair, 2 * pair + 1):
                q_ref[0, hd, :QK_NOPE, cols] = qt[hd * QK_NOPE:(hd + 1) * QK_NOPE, :].astype(BF16)
                q_ref[0, hd, QK_NOPE:, cols] = pe


def _q_proj(h, wd, g, wu, j, cos_tt, sin_tt, *, bsz, seq, tm=512, n_sub=1):
    m, d = h.shape
    tps = seq // tm
    ext_w = N_HEADS * (QK_NOPE + 2 * QK_ROPE)
    return pl.pallas_call(
        functools.partial(_q_kernel, n_sub=n_sub, j=j),
        grid=(m // tm,),
        in_specs=[
            pl.BlockSpec((tm, d), lambda i: (i, 0)),
            pl.BlockSpec((None,) + wd.shape[1:], lambda i: (j, 0, 0),
                         pipeline_mode=pl.Buffered(1)),
            pl.BlockSpec(g.shape, lambda i: (0, 0)),
            pl.BlockSpec((None,) + wu.shape[1:], lambda i: (j, 0, 0),
                         pipeline_mode=pl.Buffered(1)),
            pl.BlockSpec((LANES, tm), lambda i: (0, i % tps)),
            pl.BlockSpec((LANES, tm), lambda i: (0, i % tps)),
        ],
        out_specs=pl.BlockSpec((1, N_HEADS, QK_PAD, tm), lambda i: (i // tps, 0, 0, i % tps)),
        out_shape=jax.ShapeDtypeStruct((bsz, N_HEADS, QK_PAD, seq), BF16),
        scratch_shapes=[pltpu.VMEM(wd.shape[1:], BF16), pltpu.VMEM((Q_LORA, ext_w), BF16)],
        compiler_params=_params(1),
        name="q_proj",
    )(h, wd, g, wu, cos_tt, sin_tt)


def _attn_blocks(seq, tb, cw):
    blocks = []
    for u in range(seq // tb):
        for ch in range(seq // cw):
            shift = ch * cw - u * tb
            if shift < 0:
                continue
            if shift == 0:
                blocks.append((u * tb, cw, ch, 0))
            else:
                blocks.append((u * tb, tb, ch, None if shift >= tb - 1 else shift))
    return blocks


def _attn_kernel(q_ref, k_ref, v_ref, *refs, tk, c):
    cast_src, (o_ref, *cast_dst) = refs[:N_CAST], refs[N_CAST:]
    cw = MXU_DIM
    seq = q_ref.shape[-1]
    blocks = _attn_blocks(seq, ATTN_ROWS, cw)
    _cast_step(cast_src, cast_dst)

    def scores(hd, blk):
        row0, rows, ch, _ = blk
        return _dot(k_ref[0, hd, row0:row0 + rows, :], q_ref[0, hd, :, ch * cw:(ch + 1) * cw])

    def chain_step(hd, state, s, blk):
        row0, rows, _, mask_shift = blk
        if mask_shift is not None:
            row = lax.broadcasted_iota(jnp.int32, s.shape, 0)
            col = lax.broadcasted_iota(jnp.int32, s.shape, 1)
            s = jnp.where(row <= col + mask_shift, s, -jnp.inf)
        vt = v_ref[0, hd, row0 // tk, :, row0 % tk:row0 % tk + rows]
        m_blk = jnp.max(s, axis=0, keepdims=True)
        if state is None:
            m_new = m_blk
            p = jnp.exp2(s * c - m_new * c)
            return m_new, _dot(vt, p.astype(BF16))
        m_prev, acc_prev = state
        m_new = jnp.maximum(m_prev, m_blk)
        p = jnp.exp2(s * c - m_new * c)
        corr = jnp.exp2((m_prev - m_new) * c)
        return m_new, corr * acc_prev + _dot(vt, p.astype(BF16))

    n_heads = q_ref.shape[1]
    work = [(hd, blk) for blk in blocks for hd in range(n_heads)]
    lookahead = ATTN_LOOKAHEAD * n_heads
    states = {(hd, ch): None for hd in range(n_heads) for ch in range(seq // cw)}
    pending = []
    for idx in range(len(work) + lookahead):
        if idx < len(work):
            pending.append(scores(*work[idx]))
        if idx >= lookahead:
            hd, blk = work[idx - lookahead]
            states[hd, blk[2]] = chain_step(hd, states[hd, blk[2]], pending.pop(0), blk)

    for (hd, ch), (_, acc) in states.items():
        out = (acc[:V_HEAD, :] * (1.0 / acc[V_HEAD:V_HEAD + 1, :])).T
        o_ref[0, ch * cw:(ch + 1) * cw, hd * V_HEAD:(hd + 1) * V_HEAD] = out.astype(BF16)


def _attention(qt, k, vt, cast_weights):
    bsz, nh, _, seq = qt.shape
    tk = ATTN_TK
    c = (QK_NOPE + QK_ROPE) ** -0.5 * math.log2(math.e)
    kern = functools.partial(_attn_kernel, tk=tk, c=c)
    hg = ATTN_HEADS
    ng = nh // hg
    c_in, c_out, c_shape = _cast_specs(cast_weights, bsz * ng, lambda b, h: b * ng + h)
    return pl.pallas_call(
        kern,
        grid=(bsz, ng),
        in_specs=[
            pl.BlockSpec((1, hg, QK_PAD, seq), lambda b, h: (b, h, 0, 0)),
            pl.BlockSpec((1, hg, seq, QK_PAD), lambda b, h: (b, h, 0, 0)),
            pl.BlockSpec((1, hg, seq // tk, V_EXT, tk), lambda b, h: (b, h, 0, 0, 0)),
        ] + c_in,
        out_specs=[pl.BlockSpec((1, seq, hg * V_HEAD), lambda b, h: (b, 0, h))] + c_out,
        out_shape=[jax.ShapeDtypeStruct((bsz, seq, nh * V_HEAD), BF16)] + c_shape,
        compiler_params=_params(2),
        name="attention",
    )(qt, k, vt, *[w for w, _ in cast_weights])


def _rope_tables(seq):
    inv = 1.0 / (ROPE_THETA ** (jnp.arange(0, QK_ROPE, 2, dtype=F32) / QK_ROPE))
    ang = jnp.arange(seq, dtype=F32)[:, None] * inv[None, :]
    cos, sin = jnp.cos(ang), jnp.sin(ang)
    return jnp.tile(cos, (1, 4)), jnp.tile(sin, (1, 4))


def kernel(x, ln_g, ln_b, conv_w_in, conv_w, conv_w_out, kv_w_dkv, kv_norm_g, kv_w_ukv,
           mla_w_dq, mla_q_norm_g, mla_w_uq, mla_w_o, mlp_w1, mlp_w2):
    bsz, seq, d = x.shape
    m = bsz * seq
    h = x.reshape(m, d)
    cos_t, sin_t = _rope_tables(seq)
    cos_tt, sin_tt = cos_t.T, sin_t.T


    k_all = vt_all = None
    for layer in range(DEPTH):
        if layer < N_A:
            cast = [(conv_w_out, layer), (mlp_w1, layer), (mlp_w2, layer)]
            z, wo_bf, w1_bf, w2_bf = _conv_in(h, conv_w_in, conv_w, cast, layer, seq=seq)
            h = _proj_ln(z, wo_bf, h, ln_g, ln_b, layer)
        else:
            if layer == N_A:
                k_all, vt_all = _kv_proj(h, kv_w_dkv, kv_norm_g[None, :], kv_w_ukv,
                                         cos_t, sin_t, bsz=bsz, seq=seq)
            j = layer - N_A
            qt_all = _q_proj(h, mla_w_dq, mla_q_norm_g, mla_w_uq, j, cos_tt, sin_tt,
                             bsz=bsz, seq=seq)
            cast = [(mla_w_o, j), (mlp_w1, layer), (mlp_w2, layer)]
            o, wo_bf, w1_bf, w2_bf = _attention(qt_all, k_all, vt_all, cast)
            h = _proj_ln(o.reshape(m, N_HEADS * V_HEAD), wo_bf, h, ln_g, ln_b, layer)
        h = _mlp_ln(h, w1_bf, w2_bf, layer, ln_g, ln_b)
    return h.reshape(bsz, seq, d)
```

```python
import functools
import math

import jax
import jax.numpy as jnp
from jax import lax
from jax.experimental import pallas as pl
from jax.experimental.pallas import tpu as pltpu

D_MODEL = 2048
DEPTH = 4
N_A = DEPTH // 2
N_B = DEPTH - N_A
D_FF = 4 * D_MODEL
CONV_WIDTH = 3
N_HEADS = 16
QK_NOPE = 128
QK_ROPE = 64
V_HEAD = 128
Q_LORA = 512
KV_LORA = 512
ROPE_THETA = 10000.0
ALPHA = (2 * DEPTH) ** 0.25
LN_EPS = 1e-5
RMS_EPS = 1e-6

LANES = 128
SUBLANES = 8
MXU_DIM = 256
QK_PAD = 2 * LANES
VMEM_LIMIT = 56 * 1024 * 1024

ATTN_TK = 512
ATTN_ROWS = 512
ATTN_LOOKAHEAD = 3
ATTN_HEADS = 2
V_EXT = V_HEAD + 16

F32 = jnp.float32
BF16 = jnp.bfloat16
TN_NT_DIMS = (((0,), (1,)), ((), ()))


def _params(n_axes):
    return pltpu.CompilerParams(
        dimension_semantics=("arbitrary",) * n_axes, vmem_limit_bytes=VMEM_LIMIT)


def _dot(a, b):
    return jnp.dot(a, b, preferred_element_type=F32)


def _dot_tn_nt(a, b):
    return lax.dot_general(a, b, TN_NT_DIMS, preferred_element_type=F32)


def _layer_norm(y, g, b):
    mu = jnp.mean(y, axis=-1, keepdims=True)
    yc = y - mu
    var = jnp.mean(yc * yc, axis=-1, keepdims=True)
    return yc * lax.rsqrt(var + LN_EPS) * g + b


def _rms_norm(y, g):
    return y * lax.rsqrt(jnp.mean(y * y, axis=-1, keepdims=True) + RMS_EPS) * g


def _rotate_half_cols(w):
    half = w.shape[-1] // 2
    return jnp.concatenate([-w[:, half:], w[:, :half]], axis=-1)


N_CAST = 3


def _cast_specs(weights, n_steps, step_of):
    in_specs, out_specs, out_shape = [], [], []
    for w, idx in weights:
        rows, cols = w.shape[1] // n_steps, w.shape[2]
        in_specs.append(pl.BlockSpec((None, rows, cols),
                                     lambda *g, idx=idx: (idx, step_of(*g), 0)))
        out_specs.append(pl.BlockSpec((rows, cols), lambda *g: (step_of(*g), 0)))
        out_shape.append(jax.ShapeDtypeStruct(w.shape[1:], BF16))
    return in_specs, out_specs, out_shape


def _cast_step(src_refs, dst_refs):
    for src, dst in zip(src_refs, dst_refs):
        dst[...] = src[...].astype(BF16)


def _conv_in_kernel(h_ref, wb_ref, wc_ref, wu_ref, cw_ref, *refs, tm, tn, tiles_per_seq):
    cast_src, (z_ref, *cast_dst), (w_bf, vbuf) = (
        refs[:N_CAST], refs[N_CAST:2 * N_CAST + 1], refs[2 * N_CAST + 1:])
    i = pl.program_id(1)
    _cast_step(cast_src, cast_dst)

    @pl.when(i == 0)
    def _():
        w_bf[:, :tn] = wb_ref[...].astype(BF16)
        w_bf[:, tn:2 * tn] = wc_ref[...].astype(BF16)
        w_bf[:, 2 * tn:] = wu_ref[...].astype(BF16)

    @pl.when(i % tiles_per_seq == 0)
    def _():
        vbuf[0:SUBLANES, :] = jnp.zeros((SUBLANES, tn), F32)

    @pl.when(i % tiles_per_seq != 0)
    def _():
        vbuf[0:SUBLANES, :] = vbuf[tm:tm + SUBLANES, :]

    bcu = _dot(h_ref[...].astype(BF16), w_bf[...])
    gate_b = bcu[:, :tn]
    v = bcu[:, tn:2 * tn] * bcu[:, 2 * tn:]
    vbuf[SUBLANES:tm + SUBLANES, :] = v
    cw = cw_ref[...]
    y = (cw[2:3, :] * v
         + cw[1:2, :] * vbuf[SUBLANES - 1:tm + SUBLANES - 1, :]
         + cw[0:1, :] * vbuf[SUBLANES - 2:tm + SUBLANES - 2, :])
    z_ref[...] = (gate_b * y).astype(BF16)


def _conv_in(h, w_in, conv_w, cast_weights, layer, *, seq, tm=512, tn=512):
    m, d = h.shape
    nj, ni = d // tn, m // tm
    kern = functools.partial(_conv_in_kernel, tm=tm, tn=tn, tiles_per_seq=seq // tm)
    w_spec = [pl.BlockSpec((None, d, tn), lambda j, i, g=g: (layer, 0, g * nj + j))
              for g in range(3)]
    c_in, c_out, c_shape = _cast_specs(cast_weights, nj * ni, lambda j, i: j * ni + i)
    return pl.pallas_call(
        kern,
        grid=(nj, ni),
        in_specs=[pl.BlockSpec((tm, d), lambda j, i: (i, 0))] + w_spec + [
            pl.BlockSpec((None, CONV_WIDTH, tn), lambda j, i: (layer, 0, j))] + c_in,
        out_specs=[pl.BlockSpec((tm, tn), lambda j, i: (i, j))] + c_out,
        out_shape=[jax.ShapeDtypeStruct((m, d), BF16)] + c_shape,
        scratch_shapes=[pltpu.VMEM((d, 3 * tn), BF16),
                        pltpu.VMEM((tm + SUBLANES, tn), F32)],
        compiler_params=_params(2),
        name="conv_in",
    )(h, w_in, w_in, w_in, conv_w, *[w for w, _ in cast_weights])


def _proj_ln_kernel(z_ref, w_ref, h_ref, g_ref, b_ref, o_ref, *, n_sub):
    ts = o_ref.shape[0] // n_sub
    w = w_ref[...]
    acc = [_dot(z_ref[s * ts:(s + 1) * ts, :], w) for s in range(n_sub)]
    for s in range(n_sub):
        rows = slice(s * ts, (s + 1) * ts)
        y = ALPHA * h_ref[rows, :] + acc[s]
        o_ref[rows, :] = _layer_norm(y, g_ref[0:1, :], b_ref[0:1, :])


def _proj_ln(z, w, h, ln_g, ln_b, layer, *, tm=512, n_sub=2):
    m, k = z.shape
    d = w.shape[-1]
    return pl.pallas_call(
        functools.partial(_proj_ln_kernel, n_sub=n_sub),
        grid=(m // tm,),
        in_specs=[
            pl.BlockSpec((tm, k), lambda i: (i, 0)),
            pl.BlockSpec((k, d), lambda i: (0, 0), pipeline_mode=pl.Buffered(1)),
            pl.BlockSpec((tm, d), lambda i: (i, 0)),
            pl.BlockSpec((None, 2, d), lambda i: (layer, 0, 0)),
            pl.BlockSpec((None, 2, d), lambda i: (layer, 0, 0)),
        ],
        out_specs=pl.BlockSpec((tm, d), lambda i: (i, 0)),
        out_shape=jax.ShapeDtypeStruct((m, d), F32),
        compiler_params=_params(1),
        name="proj_ln",
    )(z, w, h, ln_g, ln_b)


def _mlp_ln_kernel(h_hbm, w1_ref, w2_ref, g_ref, b_ref, o_ref, xb_ref, h_buf, h_sem,
                   *, ln_rows):
    i = pl.program_id(0)
    k = pl.program_id(1)
    tm = o_ref.shape[0]

    def h_copy(tile):
        rows = pl.ds(pl.multiple_of(tile * tm, tm), tm)
        return pltpu.make_async_copy(h_hbm.at[rows, :], h_buf, h_sem)

    @pl.when((i == 0) & (k == 0))
    def _():
        h_copy(0).start()

    @pl.when(k == 0)
    def _():
        h_copy(i).wait()
        xb_ref[...] = h_buf[...].astype(BF16)
        o_ref[...] = ALPHA * h_buf[...]

    @pl.when((k == 1) & (i + 1 < pl.num_programs(0)))
    def _():
        h_copy(i + 1).start()

    a = jnp.maximum(_dot(xb_ref[...], w1_ref[...]), 0.0)
    o_ref[...] += _dot((a * a).astype(BF16), w2_ref[...])

    @pl.when(k == pl.num_programs(1) - 1)
    def _():
        def ln_chunk(r, carry):
            rows = pl.ds(pl.multiple_of(r * ln_rows, ln_rows), ln_rows)
            o_ref[rows, :] = _layer_norm(o_ref[rows, :], g_ref[1:2, :], b_ref[1:2, :])
            return carry

        lax.fori_loop(0, tm // ln_rows, ln_chunk, 0)


def _mlp_ln(h, w1, w2, layer, ln_g, ln_b, *, tm=1024, tf=1024, ln_rows=256):
    m, d = h.shape
    f = w1.shape[-1]
    return pl.pallas_call(
        functools.partial(_mlp_ln_kernel, ln_rows=ln_rows),
        grid=(m // tm, f // tf),
        in_specs=[
            pl.BlockSpec(memory_space=pl.ANY),
            pl.BlockSpec((d, tf), lambda i, k: (0, k)),
            pl.BlockSpec((tf, d), lambda i, k: (k, 0)),
            pl.BlockSpec((None, 2, d), lambda i, k: (layer, 0, 0)),
            pl.BlockSpec((None, 2, d), lambda i, k: (layer, 0, 0)),
        ],
        out_specs=pl.BlockSpec((tm, d), lambda i, k: (i, 0)),
        out_shape=jax.ShapeDtypeStruct((m, d), F32),
        scratch_shapes=[pltpu.VMEM((tm, d), BF16), pltpu.VMEM((tm, d), F32),
                        pltpu.SemaphoreType.DMA(())],
        compiler_params=_params(2),
        name="mlp_ln",
    )(h, w1, w2, ln_g, ln_b)


def _kv_kernel(h_ref, wd_ref, g_ref, wu_ref, cos_ref, sin_ref, k_ref, v_ref,
               wd_bf, wuk_bf, wuv_bf):
    @pl.when(pl.program_id(0) == 0)
    def _():
        wd_bf[:, :KV_LORA] = wd_ref[:, :KV_LORA].astype(BF16)
        pe = wd_ref[:, KV_LORA:]
        rot = _rotate_half_cols(pe)
        for rep in range(2):
            lo = KV_LORA + rep * QK_ROPE
            wd_bf[:, lo:lo + QK_ROPE] = pe.astype(BF16)
            wd_bf[:, lo + LANES:lo + LANES + QK_ROPE] = rot.astype(BF16)
        for hd in range(N_HEADS):
            base = hd * (QK_NOPE + V_HEAD)
            wuk_bf[:, hd * QK_NOPE:(hd + 1) * QK_NOPE] = (
                wu_ref[:, base:base + QK_NOPE].astype(BF16))
            wuv_bf[:, hd * V_HEAD:(hd + 1) * V_HEAD] = (
                wu_ref[:, base + QK_NOPE:base + QK_NOPE + V_HEAD].astype(BF16))

    tm = h_ref.shape[0]
    ckv = _dot(h_ref[...].astype(BF16), wd_bf[...])
    c = _rms_norm(ckv[:, :KV_LORA], g_ref[...]).astype(BF16)
    kn = _dot(c, wuk_bf[...])
    vt = _dot_tn_nt(wuv_bf[...], c)
    pe = ckv[:, KV_LORA:KV_LORA + LANES] * cos_ref[...] + ckv[:, KV_LORA + LANES:] * sin_ref[...]
    lane = lax.broadcasted_iota(jnp.int32, pe.shape, 1)
    pe_lo = jnp.where(lane < QK_ROPE, pe, 0.0).astype(BF16)
    pe_hi = jnp.where(lane >= QK_ROPE, pe, 0.0).astype(BF16)
    for hd in range(N_HEADS):
        k_ref[0, hd, :, :QK_NOPE] = kn[:, hd * QK_NOPE:(hd + 1) * QK_NOPE].astype(BF16)
        k_ref[0, hd, :, QK_NOPE:] = pe_lo if hd % 2 == 0 else pe_hi
        v_ref[0, hd, 0, :V_HEAD, :] = vt[hd * V_HEAD:(hd + 1) * V_HEAD, :].astype(BF16)
        v_ref[0, hd, 0, V_HEAD:, :] = jnp.ones((V_EXT - V_HEAD, tm), BF16)


def _kv_proj(h, wd, g, wu, cos_t, sin_t, *, bsz, seq):
    m, d = h.shape
    tm = ATTN_TK
    tps = seq // tm
    return pl.pallas_call(
        _kv_kernel,
        grid=(m // tm,),
        in_specs=[
            pl.BlockSpec((tm, d), lambda i: (i, 0)),
            pl.BlockSpec(wd.shape, lambda i: (0, 0), pipeline_mode=pl.Buffered(1)),
            pl.BlockSpec((1, KV_LORA), lambda i: (0, 0)),
            pl.BlockSpec(wu.shape, lambda i: (0, 0), pipeline_mode=pl.Buffered(1)),
            pl.BlockSpec((tm, LANES), lambda i: (i % tps, 0)),
            pl.BlockSpec((tm, LANES), lambda i: (i % tps, 0)),
        ],
        out_specs=[
            pl.BlockSpec((1, N_HEADS, tm, QK_PAD), lambda i: (i // tps, 0, i % tps, 0)),
            pl.BlockSpec((1, N_HEADS, 1, V_EXT, tm), lambda i: (i // tps, 0, i % tps, 0, 0)),
        ],
        out_shape=[
            jax.ShapeDtypeStruct((bsz, N_HEADS, seq, QK_PAD), BF16),
            jax.ShapeDtypeStruct((bsz, N_HEADS, tps, V_EXT, tm), BF16),
        ],
        scratch_shapes=[pltpu.VMEM((d, KV_LORA + 2 * LANES), BF16),
                        pltpu.VMEM((KV_LORA, N_HEADS * QK_NOPE), BF16),
                        pltpu.VMEM((KV_LORA, N_HEADS * V_HEAD), BF16)],
        compiler_params=_params(1),
        name="kv_proj",
    )(h, wd, g, wu, cos_t, sin_t)


def _q_kernel(h_ref, wd_ref, g_ref, wu_ref, cos_ref, sin_ref, q_ref, wd_bf, wu_bf, *, j):
    nope_w = N_HEADS * QK_NOPE
    pe_w = N_HEADS * QK_ROPE
    head_w = QK_NOPE + QK_ROPE

    @pl.when(pl.program_id(0) == 0)
    def _():
        wd_bf[...] = wd_ref[...].astype(BF16)
        for hd in range(N_HEADS):
            w = wu_ref[:, hd * head_w:(hd + 1) * head_w]
            pe = w[:, QK_NOPE:]
            rot = _rotate_half_cols(pe)
            wu_bf[:, hd * QK_NOPE:(hd + 1) * QK_NOPE] = w[:, :QK_NOPE].astype(BF16)
            wu_bf[:, nope_w + hd * QK_ROPE:nope_w + (hd + 1) * QK_ROPE] = pe.astype(BF16)
            wu_bf[:, nope_w + pe_w + hd * QK_ROPE:nope_w + pe_w + (hd + 1) * QK_ROPE] = (
                rot.astype(BF16))

    lat = _dot(h_ref[...].astype(BF16), wd_bf[...])
    qc = _rms_norm(lat, g_ref[j:j + 1, :]).astype(BF16)
    qt = _dot_tn_nt(wu_bf[...], qc)
    cos = cos_ref[...]
    sin = sin_ref[...]
    for pair in range(N_HEADS // 2):
        lo = nope_w + pair * LANES
        pe = (qt[lo:lo + LANES, :] * cos + qt[lo + pe_w:lo + pe_w + LANES, :] * sin).astype(BF16)
        for hd in (2 * pair, 2 * pair + 1):
            q_ref[0, hd, :QK_NOPE, :] = qt[hd * QK_NOPE:(hd + 1) * QK_NOPE, :].astype(BF16)
            q_ref[0, hd, QK_NOPE:, :] = pe


def _q_proj(h, wd, g, wu, j, cos_tt, sin_tt, *, bsz, seq, tm=512):
    m, d = h.shape
    tps = seq // tm
    ext_w = N_HEADS * (QK_NOPE + 2 * QK_ROPE)
    return pl.pallas_call(
        functools.partial(_q_kernel, j=j),
        grid=(m // tm,),
        in_specs=[
            pl.BlockSpec((tm, d), lambda i: (i, 0)),
            pl.BlockSpec((None,) + wd.shape[1:], lambda i: (j, 0, 0),
                         pipeline_mode=pl.Buffered(1)),
            pl.BlockSpec(g.shape, lambda i: (0, 0)),
            pl.BlockSpec((None,) + wu.shape[1:], lambda i: (j, 0, 0),
                         pipeline_mode=pl.Buffered(1)),
            pl.BlockSpec((LANES, tm), lambda i: (0, i % tps)),
            pl.BlockSpec((LANES, tm), lambda i: (0, i % tps)),
        ],
        out_specs=pl.BlockSpec((1, N_HEADS, QK_PAD, tm), lambda i: (i // tps, 0, 0, i % tps)),
        out_shape=jax.ShapeDtypeStruct((bsz, N_HEADS, QK_PAD, seq), BF16),
        scratch_shapes=[pltpu.VMEM(wd.shape[1:], BF16), pltpu.VMEM((Q_LORA, ext_w), BF16)],
        compiler_params=_params(1),
        name="q_proj",
    )(h, wd, g, wu, cos_tt, sin_tt)


def _attn_blocks(seq, tb, cw):
    blocks = []
    for u in range(seq // tb):
        for ch in range(seq // cw):
            shift = ch * cw - u * tb
            if shift < 0:
                continue
            if shift == 0:
                blocks.append((u * tb, cw, ch, 0))
            else:
                blocks.append((u * tb, tb, ch, None if shift >= tb - 1 else shift))
    return blocks


def _attn_kernel(q_ref, k_ref, v_ref, *refs, tk, c):
    cast_src, (o_ref, *cast_dst) = refs[:N_CAST], refs[N_CAST:]
    cw = MXU_DIM
    seq = q_ref.shape[-1]
    blocks = _attn_blocks(seq, ATTN_ROWS, cw)
    _cast_step(cast_src, cast_dst)

    def scores(hd, blk):
        row0, rows, ch, _ = blk
        return _dot(k_ref[0, hd, row0:row0 + rows, :], q_ref[0, hd, :, ch * cw:(ch + 1) * cw])

    def chain_step(hd, state, s, blk):
        row0, rows, _, mask_shift = blk
        if mask_shift is not None:
            row = lax.broadcasted_iota(jnp.int32, s.shape, 0)
            col = lax.broadcasted_iota(jnp.int32, s.shape, 1)
            s = jnp.where(row <= col + mask_shift, s, -jnp.inf)
        vt = v_ref[0, hd, row0 // tk, :, row0 % tk:row0 % tk + rows]
        m_blk = jnp.max(s, axis=0, keepdims=True)
        if state is None:
            m_new = m_blk
            p = jnp.exp2(s * c - m_new * c)
            return m_new, _dot(vt, p.astype(BF16))
        m_prev, acc_prev = state
        m_new = jnp.maximum(m_prev, m_blk)
        p = jnp.exp2(s * c - m_new * c)
        corr = jnp.exp2((m_prev - m_new) * c)
        return m_new, corr * acc_prev + _dot(vt, p.astype(BF16))

    n_heads = q_ref.shape[1]
    work = [(hd, blk) for blk in blocks for hd in range(n_heads)]
    lookahead = ATTN_LOOKAHEAD * n_heads
    states = {(hd, ch): None for hd in range(n_heads) for ch in range(seq // cw)}
    pending = []
    for idx in range(len(work) + lookahead):
        if idx < len(work):
            pending.append(scores(*work[idx]))
        if idx >= lookahead:
            hd, blk = work[idx - lookahead]
            states[hd, blk[2]] = chain_step(hd, states[hd, blk[2]], pending.pop(0), blk)

    for (hd, ch), (_, acc) in states.items():
        out = (acc[:V_HEAD, :] * (1.0 / acc[V_HEAD:V_HEAD + 1, :])).T
        o_ref[0, ch * cw:(ch + 1) * cw, hd * V_HEAD:(hd + 1) * V_HEAD] = out.astype(BF16)


def _attention(qt, k, vt, cast_weights):
    bsz, nh, _, seq = qt.shape
    tk = ATTN_TK
    c = (QK_NOPE + QK_ROPE) ** -0.5 * math.log2(math.e)
    kern = functools.partial(_attn_kernel, tk=tk, c=c)
    hg = ATTN_HEADS
    ng = nh // hg
    c_in, c_out, c_shape = _cast_specs(cast_weights, bsz * ng, lambda b, h: b * ng + h)
    return pl.pallas_call(
        kern,
        grid=(bsz, ng),
        in_specs=[
            pl.BlockSpec((1, hg, QK_PAD, seq), lambda b, h: (b, h, 0, 0)),
            pl.BlockSpec((1, hg, seq, QK_PAD), lambda b, h: (b, h, 0, 0)),
            pl.BlockSpec((1, hg, seq // tk, V_EXT, tk), lambda b, h: (b, h, 0, 0, 0)),
        ] + c_in,
        out_specs=[pl.BlockSpec((1, seq, hg * V_HEAD), lambda b, h: (b, 0, h))] + c_out,
        out_shape=[jax.ShapeDtypeStruct((bsz, seq, nh * V_HEAD), BF16)] + c_shape,
        compiler_params=_params(2),
        name="attention",
    )(qt, k, vt, *[w for w, _ in cast_weights])


def _rope_tables(seq):
    inv = 1.0 / (ROPE_THETA ** (jnp.arange(0, QK_ROPE, 2, dtype=F32) / QK_ROPE))
    ang = jnp.arange(seq, dtype=F32)[:, None] * inv[None, :]
    cos, sin = jnp.cos(ang), jnp.sin(ang)
    return jnp.tile(cos, (1, 4)), jnp.tile(sin, (1, 4))


def kernel(x, ln_g, ln_b, conv_w_in, conv_w, conv_w_out, kv_w_dkv, kv_norm_g, kv_w_ukv,
           mla_w_dq, mla_q_norm_g, mla_w_uq, mla_w_o, mlp_w1, mlp_w2):
    bsz, seq, d = x.shape
    m = bsz * seq
    h = x.reshape(m, d)
    cos_t, sin_t = _rope_tables(seq)
    cos_tt, sin_tt = cos_t.T, sin_t.T


    k_all = vt_all = None
    for layer in range(DEPTH):
        if layer < N_A:
            cast = [(conv_w_out, layer), (mlp_w1, layer), (mlp_w2, layer)]
            z, wo_bf, w1_bf, w2_bf = _conv_in(h, conv_w_in, conv_w, cast, layer, seq=seq)
            h = _proj_ln(z, wo_bf, h, ln_g, ln_b, layer)
        else:
            if layer == N_A:
                k_all, vt_all = _kv_proj(h, kv_w_dkv, kv_norm_g[None, :], kv_w_ukv,
                                         cos_t, sin_t, bsz=bsz, seq=seq)
            j = layer - N_A
            qt_all = _q_proj(h, mla_w_dq, mla_q_norm_g, mla_w_uq, j, cos_tt, sin_tt,
                             bsz=bsz, seq=seq)
            cast = [(mla_w_o, j), (mlp_w1, layer), (mlp_w2, layer)]
            o, wo_bf, w1_bf, w2_bf = _attention(qt_all, k_all, vt_all, cast)
            h = _proj_ln(o.reshape(m, N_HEADS * V_HEAD), wo_bf, h, ln_g, ln_b, layer)
        h = _mlp_ln(h, w1_bf, w2_bf, layer, ln_g, ln_b)
    return h.reshape(bsz, seq, d)
```

```python
import functools
import math

import jax
import jax.numpy as jnp
from jax import lax
from jax.experimental import pallas as pl
from jax.experimental.pallas import tpu as pltpu

DEPTH = 4
N_A = DEPTH // 2
CONV_WIDTH = 3
N_HEADS = 16
QK_NOPE = 128
QK_ROPE = 64
V_HEAD = 128
Q_LORA = 512
KV_LORA = 512
ROPE_THETA = 10000.0
ALPHA = (2 * DEPTH) ** 0.25
LN_EPS = 1e-5
RMS_EPS = 1e-6

LANES = 128
SUBLANES = 8
MXU_DIM = 256
QK_PAD = 2 * LANES
VMEM_LIMIT = 56 * 1024 * 1024
MLP_VMEM_LIMIT = 62 * 1024 * 1024

ATTN_TK = 512
ATTN_ROWS = 512
ATTN_LOOKAHEAD = 2
ATTN_HEADS = 2
V_EXT = V_HEAD + 16

F32 = jnp.float32
BF16 = jnp.bfloat16
TN_NT_DIMS = (((0,), (1,)), ((), ()))


def _params(n_axes, vmem_limit=VMEM_LIMIT):
    return pltpu.CompilerParams(
        dimension_semantics=("arbitrary",) * n_axes, vmem_limit_bytes=vmem_limit)


def _dot(a, b):
    return jnp.dot(a, b, preferred_element_type=F32)


def _dot_tn_nt(a, b):
    return lax.dot_general(a, b, TN_NT_DIMS, preferred_element_type=F32)


def _layer_norm(y, g, b):
    mu = jnp.mean(y, axis=-1, keepdims=True)
    yc = y - mu
    var = jnp.mean(yc * yc, axis=-1, keepdims=True)
    return yc * lax.rsqrt(var + LN_EPS) * g + b


def _rms_norm(y, g):
    return y * lax.rsqrt(jnp.mean(y * y, axis=-1, keepdims=True) + RMS_EPS) * g


def _rotate_half_cols(w):
    half = w.shape[-1] // 2
    return jnp.concatenate([-w[:, half:], w[:, :half]], axis=-1)


N_CAST = 3


def _cast_specs(weights, n_steps, step_of):
    in_specs, out_specs, out_shape = [], [], []
    for w, idx in weights:
        rows, cols = w.shape[1] // n_steps, w.shape[2]
        in_specs.append(pl.BlockSpec((None, rows, cols),
                                     lambda *g, idx=idx: (idx, step_of(*g), 0)))
        out_specs.append(pl.BlockSpec((rows, cols), lambda *g: (step_of(*g), 0)))
        out_shape.append(jax.ShapeDtypeStruct(w.shape[1:], BF16))
    return in_specs, out_specs, out_shape


def _cast_step(src_refs, dst_refs):
    for src, dst in zip(src_refs, dst_refs):
        dst[...] = src[...].astype(BF16)


def _conv_in_kernel(h_ref, wb_ref, wc_ref, wu_ref, cw_ref, *refs, tm, tn, tiles_per_seq):
    cast_src, (z_ref, *cast_dst), (w_bf, vbuf) = (
        refs[:N_CAST], refs[N_CAST:2 * N_CAST + 1], refs[2 * N_CAST + 1:])
    i = pl.program_id(1)
    _cast_step(cast_src, cast_dst)

    @pl.when(i == 0)
    def _():
        w_bf[:, :tn] = wb_ref[...].astype(BF16)
        w_bf[:, tn:2 * tn] = wc_ref[...].astype(BF16)
        w_bf[:, 2 * tn:] = wu_ref[...].astype(BF16)

    @pl.when(i % tiles_per_seq == 0)
    def _():
        vbuf[0:SUBLANES, :] = jnp.zeros((SUBLANES, tn), F32)

    @pl.when(i % tiles_per_seq != 0)
    def _():
        vbuf[0:SUBLANES, :] = vbuf[tm:tm + SUBLANES, :]

    bcu = _dot(h_ref[...].astype(BF16), w_bf[...])
    gate_b = bcu[:, :tn]
    v = bcu[:, tn:2 * tn] * bcu[:, 2 * tn:]
    vbuf[SUBLANES:tm + SUBLANES, :] = v
    cw = cw_ref[...]
    y = (cw[2:3, :] * v
         + cw[1:2, :] * vbuf[SUBLANES - 1:tm + SUBLANES - 1, :]
         + cw[0:1, :] * vbuf[SUBLANES - 2:tm + SUBLANES - 2, :])
    z_ref[...] = (gate_b * y).astype(BF16)


def _conv_in(h, w_in, conv_w, cast_weights, layer, *, seq, tm=512, tn=512):
    m, d = h.shape
    nj, ni = d // tn, m // tm
    kern = functools.partial(_conv_in_kernel, tm=tm, tn=tn, tiles_per_seq=seq // tm)
    w_spec = [pl.BlockSpec((None, d, tn), lambda j, i, g=g: (layer, 0, g * nj + j))
              for g in range(3)]
    c_in, c_out, c_shape = _cast_specs(cast_weights, nj * ni, lambda j, i: j * ni + i)
    return pl.pallas_call(
        kern,
        grid=(nj, ni),
        in_specs=[pl.BlockSpec((tm, d), lambda j, i: (i, 0))] + w_spec + [
            pl.BlockSpec((None, CONV_WIDTH, tn), lambda j, i: (layer, 0, j))] + c_in,
        out_specs=[pl.BlockSpec((tm, tn), lambda j, i: (i, j))] + c_out,
        out_shape=[jax.ShapeDtypeStruct((m, d), BF16)] + c_shape,
        scratch_shapes=[pltpu.VMEM((d, 3 * tn), BF16),
                        pltpu.VMEM((tm + SUBLANES, tn), F32)],
        compiler_params=_params(2),
        name="conv_in",
    )(h, w_in, w_in, w_in, conv_w, *[w for w, _ in cast_weights])


def _proj_ln_kernel(z_ref, w_ref, h_ref, g_ref, b_ref, o_ref, *, n_sub):
    ts = o_ref.shape[0] // n_sub
    w = w_ref[...]
    acc = [_dot(z_ref[s * ts:(s + 1) * ts, :], w) for s in range(n_sub)]
    for s in range(n_sub):
        rows = slice(s * ts, (s + 1) * ts)
        y = ALPHA * h_ref[rows, :] + acc[s]
        o_ref[rows, :] = _layer_norm(y, g_ref[0:1, :], b_ref[0:1, :])


def _proj_ln(z, w, h, ln_g, ln_b, layer, *, tm=512, n_sub=2):
    m, k = z.shape
    d = w.shape[-1]
    return pl.pallas_call(
        functools.partial(_proj_ln_kernel, n_sub=n_sub),
        grid=(m // tm,),
        in_specs=[
            pl.BlockSpec((tm, k), lambda i: (i, 0)),
            pl.BlockSpec((k, d), lambda i: (0, 0), pipeline_mode=pl.Buffered(1)),
            pl.BlockSpec((tm, d), lambda i: (i, 0)),
            pl.BlockSpec((None, 2, d), lambda i: (layer, 0, 0)),
            pl.BlockSpec((None, 2, d), lambda i: (layer, 0, 0)),
        ],
        out_specs=pl.BlockSpec((tm, d), lambda i: (i, 0)),
        out_shape=jax.ShapeDtypeStruct((m, d), F32),
        compiler_params=_params(1),
        name="proj_ln",
    )(z, w, h, ln_g, ln_b)


def _mlp_ln_kernel(h_hbm, w1_ref, w2_ref, g_ref, b_ref, o_ref, *refs, ln_rows, emit_bf16):
    (ob_ref,), (xb_ref, h_buf, h_sem) = (refs[:1], refs[1:]) if emit_bf16 else ((None,), refs)
    i = pl.program_id(0)
    k = pl.program_id(1)
    tm = o_ref.shape[0]

    def h_copy(tile):
        rows = pl.ds(pl.multiple_of(tile * tm, tm), tm)
        return pltpu.make_async_copy(h_hbm.at[rows, :], h_buf, h_sem)

    @pl.when((i == 0) & (k == 0))
    def _():
        h_copy(0).start()

    @pl.when(k == 0)
    def _():
        h_copy(i).wait()
        xb_ref[...] = h_buf[...].astype(BF16)
        o_ref[...] = ALPHA * h_buf[...]

    @pl.when((k == 1) & (i + 1 < pl.num_programs(0)))
    def _():
        h_copy(i + 1).start()

    a = jnp.maximum(_dot(xb_ref[...], w1_ref[...]), 0.0)
    o_ref[...] += _dot((a * a).astype(BF16), w2_ref[...])

    @pl.when(k == pl.num_programs(1) - 1)
    def _():
        def ln_chunk(r, carry):
            rows = pl.ds(pl.multiple_of(r * ln_rows, ln_rows), ln_rows)
            y = _layer_norm(o_ref[rows, :], g_ref[1:2, :], b_ref[1:2, :])
            o_ref[rows, :] = y
            if emit_bf16:
                ob_ref[rows, :] = y.astype(BF16)
            return carry

        lax.fori_loop(0, tm // ln_rows, ln_chunk, 0)


def _mlp_ln(h, w1, w2, layer, ln_g, ln_b, *, emit_bf16, tm=1024, tf=1024, ln_rows=512):
    m, d = h.shape
    f = w1.shape[-1]
    out_spec = pl.BlockSpec((tm, d), lambda i, k: (i, 0))
    out_specs = [out_spec, out_spec] if emit_bf16 else out_spec
    out_f32 = jax.ShapeDtypeStruct((m, d), F32)
    out_shape = [out_f32, jax.ShapeDtypeStruct((m, d), BF16)] if emit_bf16 else out_f32
    return pl.pallas_call(
        functools.partial(_mlp_ln_kernel, ln_rows=ln_rows, emit_bf16=emit_bf16),
        grid=(m // tm, f // tf),
        in_specs=[
            pl.BlockSpec(memory_space=pl.ANY),
            pl.BlockSpec((d, tf), lambda i, k: (0, k)),
            pl.BlockSpec((tf, d), lambda i, k: (k, 0)),
            pl.BlockSpec((None, 2, d), lambda i, k: (layer, 0, 0)),
            pl.BlockSpec((None, 2, d), lambda i, k: (layer, 0, 0)),
        ],
        out_specs=out_specs,
        out_shape=out_shape,
        scratch_shapes=[pltpu.VMEM((tm, d), BF16), pltpu.VMEM((tm, d), F32),
                        pltpu.SemaphoreType.DMA(())],
        compiler_params=_params(2, MLP_VMEM_LIMIT),
        name="mlp_ln",
    )(h, w1, w2, ln_g, ln_b)


def _kv_kernel(h_ref, wd_ref, g_ref, wu_ref, cos_ref, sin_ref, k_ref, v_ref,
               wd_bf, wuk_bf, wuv_bf):
    @pl.when(pl.program_id(0) == 0)
    def _():
        wd_bf[:, :KV_LORA] = wd_ref[:, :KV_LORA].astype(BF16)
        pe = wd_ref[:, KV_LORA:]
        rot = _rotate_half_cols(pe)
        for rep in range(2):
            lo = KV_LORA + rep * QK_ROPE
            wd_bf[:, lo:lo + QK_ROPE] = pe.astype(BF16)
            wd_bf[:, lo + LANES:lo + LANES + QK_ROPE] = rot.astype(BF16)
        for hd in range(N_HEADS):
            base = hd * (QK_NOPE + V_HEAD)
            wuk_bf[:, hd * QK_NOPE:(hd + 1) * QK_NOPE] = (
                wu_ref[:, base:base + QK_NOPE].astype(BF16))
            wuv_bf[:, hd * V_HEAD:(hd + 1) * V_HEAD] = (
                wu_ref[:, base + QK_NOPE:base + QK_NOPE + V_HEAD].astype(BF16))

    tm = h_ref.shape[0]
    ckv = _dot(h_ref[...].astype(BF16), wd_bf[...])
    c = _rms_norm(ckv[:, :KV_LORA], g_ref[...]).astype(BF16)
    kn = _dot(c, wuk_bf[...])
    vt = _dot_tn_nt(wuv_bf[...], c)
    pe = ckv[:, KV_LORA:KV_LORA + LANES] * cos_ref[...] + ckv[:, KV_LORA + LANES:] * sin_ref[...]
    lane = lax.broadcasted_iota(jnp.int32, pe.shape, 1)
    pe_lo = jnp.where(lane < QK_ROPE, pe, 0.0).astype(BF16)
    pe_hi = jnp.where(lane >= QK_ROPE, pe, 0.0).astype(BF16)
    for hd in range(N_HEADS):
        k_ref[0, hd, :, :QK_NOPE] = kn[:, hd * QK_NOPE:(hd + 1) * QK_NOPE].astype(BF16)
        k_ref[0, hd, :, QK_NOPE:] = pe_lo if hd % 2 == 0 else pe_hi
        v_ref[0, hd, 0, :V_HEAD, :] = vt[hd * V_HEAD:(hd + 1) * V_HEAD, :].astype(BF16)
        v_ref[0, hd, 0, V_HEAD:, :] = jnp.ones((V_EXT - V_HEAD, tm), BF16)


def _kv_proj(h, wd, g, wu, cos_t, sin_t, *, bsz, seq):
    m, d = h.shape
    tm = ATTN_TK
    tps = seq // tm
    return pl.pallas_call(
        _kv_kernel,
        grid=(m // tm,),
        in_specs=[
            pl.BlockSpec((tm, d), lambda i: (i, 0)),
            pl.BlockSpec(wd.shape, lambda i: (0, 0), pipeline_mode=pl.Buffered(1)),
            pl.BlockSpec((1, KV_LORA), lambda i: (0, 0)),
            pl.BlockSpec(wu.shape, lambda i: (0, 0), pipeline_mode=pl.Buffered(1)),
            pl.BlockSpec((tm, LANES), lambda i: (i % tps, 0)),
            pl.BlockSpec((tm, LANES), lambda i: (i % tps, 0)),
        ],
        out_specs=[
            pl.BlockSpec((1, N_HEADS, tm, QK_PAD), lambda i: (i // tps, 0, i % tps, 0)),
            pl.BlockSpec((1, N_HEADS, 1, V_EXT, tm), lambda i: (i // tps, 0, i % tps, 0, 0)),
        ],
        out_shape=[
            jax.ShapeDtypeStruct((bsz, N_HEADS, seq, QK_PAD), BF16),
            jax.ShapeDtypeStruct((bsz, N_HEADS, tps, V_EXT, tm), BF16),
        ],
        scratch_shapes=[pltpu.VMEM((d, KV_LORA + 2 * LANES), BF16),
                        pltpu.VMEM((KV_LORA, N_HEADS * QK_NOPE), BF16),
                        pltpu.VMEM((KV_LORA, N_HEADS * V_HEAD), BF16)],
        compiler_params=_params(1),
        name="kv_proj",
    )(h, wd, g, wu, cos_t, sin_t)


def _q_kernel(h_ref, wd_ref, g_ref, wu_ref, cos_ref, sin_ref, q_ref, wd_bf, wu_bf, *, j):
    nope_w = N_HEADS * QK_NOPE
    pe_w = N_HEADS * QK_ROPE
    head_w = QK_NOPE + QK_ROPE

    @pl.when(pl.program_id(0) == 0)
    def _():
        wd_bf[...] = wd_ref[...].astype(BF16)
        for hd in range(N_HEADS):
            w = wu_ref[:, hd * head_w:(hd + 1) * head_w]
            pe = w[:, QK_NOPE:]
            rot = _rotate_half_cols(pe)
            wu_bf[:, hd * QK_NOPE:(hd + 1) * QK_NOPE] = w[:, :QK_NOPE].astype(BF16)
            wu_bf[:, nope_w + hd * QK_ROPE:nope_w + (hd + 1) * QK_ROPE] = pe.astype(BF16)
            wu_bf[:, nope_w + pe_w + hd * QK_ROPE:nope_w + pe_w + (hd + 1) * QK_ROPE] = (
                rot.astype(BF16))

    lat = _dot(h_ref[...].astype(BF16), wd_bf[...])
    qc = _rms_norm(lat, g_ref[j:j + 1, :]).astype(BF16)
    qt = _dot_tn_nt(wu_bf[...], qc)
    cos = cos_ref[...]
    sin = sin_ref[...]
    for pair in range(N_HEADS // 2):
        lo = nope_w + pair * LANES
        pe = (qt[lo:lo + LANES, :] * cos + qt[lo + pe_w:lo + pe_w + LANES, :] * sin).astype(BF16)
        for hd in (2 * pair, 2 * pair + 1):
            q_ref[0, hd, :QK_NOPE, :] = qt[hd * QK_NOPE:(hd + 1) * QK_NOPE, :].astype(BF16)
            q_ref[0, hd, QK_NOPE:, :] = pe


def _q_proj(h, wd, g, wu, j, cos_tt, sin_tt, *, bsz, seq, tm=512):
    m, d = h.shape
    tps = seq // tm
    ext_w = N_HEADS * (QK_NOPE + 2 * QK_ROPE)
    return pl.pallas_call(
        functools.partial(_q_kernel, j=j),
        grid=(m // tm,),
        in_specs=[
            pl.BlockSpec((tm, d), lambda i: (i, 0)),
            pl.BlockSpec((None,) + wd.shape[1:], lambda i: (j, 0, 0),
                         pipeline_mode=pl.Buffered(1)),
            pl.BlockSpec(g.shape, lambda i: (0, 0)),
            pl.BlockSpec((None,) + wu.shape[1:], lambda i: (j, 0, 0),
                         pipeline_mode=pl.Buffered(1)),
            pl.BlockSpec((LANES, tm), lambda i: (0, i % tps)),
            pl.BlockSpec((LANES, tm), lambda i: (0, i % tps)),
        ],
        out_specs=pl.BlockSpec((1, N_HEADS, QK_PAD, tm), lambda i: (i // tps, 0, 0, i % tps)),
        out_shape=jax.ShapeDtypeStruct((bsz, N_HEADS, QK_PAD, seq), BF16),
        scratch_shapes=[pltpu.VMEM(wd.shape[1:], BF16), pltpu.VMEM((Q_LORA, ext_w), BF16)],
        compiler_params=_params(1),
        name="q_proj",
    )(h, wd, g, wu, cos_tt, sin_tt)


def _attn_blocks(seq, tb, cw):
    blocks = []
    for u in range(seq // tb):
        for ch in range(seq // cw):
            shift = ch * cw - u * tb
            if shift < 0:
                continue
            if shift == 0:
                blocks.append((u * tb, cw, ch, 0))
            else:
                blocks.append((u * tb, tb, ch, None if shift >= tb - 1 else shift))
    return blocks


def _attn_kernel(q_ref, k_ref, v_ref, *refs, tk, c):
    cast_src, (o_ref, *cast_dst) = refs[:N_CAST], refs[N_CAST:]
    cw = MXU_DIM
    seq = q_ref.shape[-1]
    blocks = _attn_blocks(seq, ATTN_ROWS, cw)
    _cast_step(cast_src, cast_dst)

    def scores(hd, blk):
        row0, rows, ch, _ = blk
        return _dot(k_ref[0, hd, row0:row0 + rows, :], q_ref[0, hd, :, ch * cw:(ch + 1) * cw])

    def chain_step(hd, state, s, blk):
        row0, rows, _, mask_shift = blk
        if mask_shift is not None:
            row = lax.broadcasted_iota(jnp.int32, s.shape, 0)
            col = lax.broadcasted_iota(jnp.int32, s.shape, 1)
            s = jnp.where(row <= col + mask_shift, s, -jnp.inf)
        vt = v_ref[0, hd, row0 // tk, :, row0 % tk:row0 % tk + rows]
        m_blk = jnp.max(s, axis=0, keepdims=True)
        if state is None:
            m_new = m_blk
            p = jnp.exp2(s * c - m_new * c)
            return m_new, _dot(vt, p.astype(BF16))
        m_prev, acc_prev = state
        m_new = jnp.maximum(m_prev, m_blk)
        p = jnp.exp2(s * c - m_new * c)
        corr = jnp.exp2((m_prev - m_new) * c)
        return m_new, corr * acc_prev + _dot(vt, p.astype(BF16))

    n_heads = q_ref.shape[1]
    work = [(hd, blk) for blk in blocks for hd in range(n_heads)]
    lookahead = ATTN_LOOKAHEAD * n_heads
    states = {(hd, ch): None for hd in range(n_heads) for ch in range(seq // cw)}
    pending = []
    for idx in range(len(work) + lookahead):
        if idx < len(work):
            pending.append(scores(*work[idx]))
        if idx >= lookahead:
            hd, blk = work[idx - lookahead]
            states[hd, blk[2]] = chain_step(hd, states[hd, blk[2]], pending.pop(0), blk)

    for (hd, ch), (_, acc) in states.items():
        out = (acc[:V_HEAD, :] * (1.0 / acc[V_HEAD:V_HEAD + 1, :])).T
        o_ref[0, ch * cw:(ch + 1) * cw, hd * V_HEAD:(hd + 1) * V_HEAD] = out.astype(BF16)


def _attention(qt, k, vt, cast_weights):
    bsz, nh, _, seq = qt.shape
    tk = ATTN_TK
    c = (QK_NOPE + QK_ROPE) ** -0.5 * math.log2(math.e)
    kern = functools.partial(_attn_kernel, tk=tk, c=c)
    hg = ATTN_HEADS
    ng = nh // hg
    c_in, c_out, c_shape = _cast_specs(cast_weights, bsz * ng, lambda b, h: b * ng + h)
    return pl.pallas_call(
        kern,
        grid=(bsz, ng),
        in_specs=[
            pl.BlockSpec((1, hg, QK_PAD, seq), lambda b, h: (b, h, 0, 0)),
            pl.BlockSpec((1, hg, seq, QK_PAD), lambda b, h: (b, h, 0, 0)),
            pl.BlockSpec((1, hg, seq // tk, V_EXT, tk), lambda b, h: (b, h, 0, 0, 0)),
        ] + c_in,
        out_specs=[pl.BlockSpec((1, seq, hg * V_HEAD), lambda b, h: (b, 0, h))] + c_out,
        out_shape=[jax.ShapeDtypeStruct((bsz, seq, nh * V_HEAD), BF16)] + c_shape,
        compiler_params=_params(2),
        name="attention",
    )(qt, k, vt, *[w for w, _ in cast_weights])


def _rope_tables(seq):
    inv = 1.0 / (ROPE_THETA ** (jnp.arange(0, QK_ROPE, 2, dtype=F32) / QK_ROPE))
    ang = jnp.arange(seq, dtype=F32)[:, None] * inv[None, :]
    cos, sin = jnp.cos(ang), jnp.sin(ang)
    return jnp.tile(cos, (1, 4)), jnp.tile(sin, (1, 4))


def kernel(x, ln_g, ln_b, conv_w_in, conv_w, conv_w_out, kv_w_dkv, kv_norm_g, kv_w_ukv,
           mla_w_dq, mla_q_norm_g, mla_w_uq, mla_w_o, mlp_w1, mlp_w2):
    bsz, seq, d = x.shape
    m = bsz * seq
    h = x.reshape(m, d)
    cos_t, sin_t = _rope_tables(seq)
    cos_tt, sin_tt = cos_t.T, sin_t.T

    h_mm = h
    k_all = vt_all = None
    for layer in range(DEPTH):
        if layer < N_A:
            cast = [(conv_w_out, layer), (mlp_w1, layer), (mlp_w2, layer)]
            z, wo_bf, w1_bf, w2_bf = _conv_in(h_mm, conv_w_in, conv_w, cast, layer, seq=seq)
            h = _proj_ln(z, wo_bf, h, ln_g, ln_b, layer)
        else:
            if layer == N_A:
                k_all, vt_all = _kv_proj(h_mm, kv_w_dkv, kv_norm_g[None, :], kv_w_ukv,
                                         cos_t, sin_t, bsz=bsz, seq=seq)
            j = layer - N_A
            qt_all = _q_proj(h_mm, mla_w_dq, mla_q_norm_g, mla_w_uq, j, cos_tt, sin_tt,
                             bsz=bsz, seq=seq)
            cast = [(mla_w_o, j), (mlp_w1, layer), (mlp_w2, layer)]
            o, wo_bf, w1_bf, w2_bf = _attention(qt_all, k_all, vt_all, cast)
            h = _proj_ln(o.reshape(m, N_HEADS * V_HEAD), wo_bf, h, ln_g, ln_b, layer)
        if layer + 1 < DEPTH:
            h, h_mm = _mlp_ln(h, w1_bf, w2_bf, layer, ln_g, ln_b, emit_bf16=True)
        else:
            h = _mlp_ln(h, w1_bf, w2_bf, layer, ln_g, ln_b, emit_bf16=False)
    return h.reshape(bsz, seq, d)
```
